```python
import math
import jax, jax.numpy as jnp
from jax import lax
import numpy as np

D_MODEL = 2048
BATCH = 4
SEQ = 8192
DEPTH = 4

HEAD_DIM = 128
D_MIX = D_MODEL
N_GROUPS = 4
GW = D_MIX // N_GROUPS
GH = GW // HEAD_DIM
D_FF = 11 * D_MODEL // 4
DILATED_BRANCHES = ((128, 1), (512, 4), (2048, 16))
Q_BLOCK = 128
CHUNK = 64
CONV_WIDTH = 4
N_MOD = 9
EPS = 1e-6
NEG_BIG = -1e30
LB_FLOOR = 1e-30
IN_SIZES = (GW,) * 10 + (3 * GW, GW, GH, GH)
IN_COLS = 14 * GW + 2 * GH

kernel_name = 'hybrid_parallel_group_decoder'


def _split_points(sizes):
    pts, acc = [], 0
    for s in sizes[:-1]:
        acc += s
        pts.append(acc)
    return pts


def rmsnorm(x, gain):
    xf = x.astype(jnp.float32)
    y = xf * lax.rsqrt(jnp.mean(xf * xf, axis=-1, keepdims=True) + EPS)
    return (y * gain.astype(jnp.float32)).astype(x.dtype)


def modulate(h, shift, scale):
    return h * (1.0 + scale) + shift


def swiglu(h, w13, w2):
    gate, up = jnp.split(h @ w13, 2, axis=-1)
    return (jax.nn.silu(gate) * up) @ w2


def heads(t):
    B, S, W = t.shape
    return t.reshape(B, S, W // HEAD_DIM, HEAD_DIM).transpose(0, 2, 1, 3)


def merge_heads(t):
    B, H, S, Dh = t.shape
    return t.transpose(0, 2, 1, 3).reshape(B, S, H * Dh)


def head_rmsnorm(t, gain):
    B, S, W = t.shape
    th = t.reshape(B, S, W // HEAD_DIM, HEAD_DIM)
    th = th * lax.rsqrt(jnp.mean(th * th, axis=-1, keepdims=True) + EPS)
    return th.reshape(B, S, W) * gain.astype(jnp.float32)


def l2norm(t):
    return t * lax.rsqrt(jnp.sum(t * t, axis=-1, keepdims=True) + EPS)


def causal_depthwise_conv(x, w):
    C = x.shape[-1]
    return lax.conv_general_dilated(
        x, w[:, None, :], window_strides=(1,), padding=[(w.shape[0] - 1, 0)],
        dimension_numbers=('NWC', 'WIO', 'NWC'), feature_group_count=C)


def _to_blocks(t, n, blk):
    B, H, S, D = t.shape
    return t.reshape(B, H, n, blk, D).transpose(2, 0, 1, 3, 4)


def _from_blocks(t):
    n, B, H, blk, D = t.shape
    return t.transpose(1, 2, 0, 3, 4).reshape(B, H, n * blk, D)


def dilated_attention(q, k, v):
    B, H, S, Dh = q.shape
    nb = S // Q_BLOCK
    scale = Dh ** -0.5

    def one_block(args):
        blk, q_blk = args
        t = blk * Q_BLOCK + jnp.arange(Q_BLOCK)
        lses, outs = [], []
        for window, dil in DILATED_BRANCHES:
            offs = dil * jnp.arange(window // dil + 1)
            pos = t[:, None] - offs[None, :]
            valid = pos >= 0
            pos = jnp.maximum(pos, 0)
            kg = k[:, :, pos]
            vg = v[:, :, pos]
            s = jnp.einsum('bhqd,bhqwd->bhqw', q_blk, kg) * scale
            s = jnp.where(valid, s, NEG_BIG)
            m = jnp.max(s, axis=-1, keepdims=True)
            p = jnp.where(valid, jnp.exp(s - m), 0.0)
            l = jnp.sum(p, axis=-1, keepdims=True)
            outs.append(jnp.einsum('bhqw,bhqwd->bhqd', p, vg) / l)
            lses.append(m + jnp.log(l))
        wts = jax.nn.softmax(jnp.stack(lses), axis=0)
        return jnp.sum(wts * jnp.stack(outs), axis=0)

    o = lax.map(one_block, (jnp.arange(nb), _to_blocks(q, nb, Q_BLOCK)))
    return _from_blocks(o)


def stick_breaking_attention(q, k, v):
    B, H, S, Dh = q.shape
    nb = S // Q_BLOCK
    scale = Dh ** -0.5
    key_pos = jnp.arange(S)

    def one_block(args):
        blk, q_blk = args
        t = blk * Q_BLOCK + jnp.arange(Q_BLOCK)
        z = jnp.einsum('bhqd,bhsd->bhqs', q_blk, k) * scale
        causal = key_pos[None, :] < t[:, None]
        log_keep = jnp.where(causal, jax.nn.log_sigmoid(-z), 0.0)
        tail = lax.cumsum(log_keep, axis=3, reverse=True)
        a = jnp.where(causal, jnp.exp(jnp.where(causal, z + tail, NEG_BIG)), 0.0)
        return jnp.einsum('bhqs,bhsd->bhqd', a, v)

    o = lax.map(one_block, (jnp.arange(nb), _to_blocks(q, nb, Q_BLOCK)))
    return _from_blocks(o)


def hgrn2_recurrence(q, k, v, log_f):
    B, H, S, Dk = q.shape
    n = S // CHUNK
    incl = jnp.tril(jnp.ones((CHUNK, CHUNK), dtype=bool))[:, :, None]

    def step(state, inp):
        qc, kc, vc, lf = inp
        b = jnp.cumsum(lf, axis=2)
        rel = b[:, :, :, None, :] - b[:, :, None, :, :]
        decay = jnp.where(incl, jnp.exp(jnp.where(incl, rel, 0.0)), 0.0)
        scores = jnp.einsum('bhtc,bhtsc,bhsc->bhts', qc, decay, kc)
        o = (jnp.einsum('bhtc,bhcv->bhtv', qc * jnp.exp(b), state)
             + jnp.einsum('bhts,bhsv->bhtv', scores, vc))
        b_last = b[:, :, -1:, :]
        state = (jnp.exp(b[:, :, -1, :])[..., None] * state
                 + jnp.einsum('bhsc,bhsv->bhcv', kc * jnp.exp(b_last - b), vc))
        return state, o

    state0 = jnp.zeros((B, H, Dk, v.shape[-1]), jnp.float32)
    _, o = lax.scan(step, state0, (_to_blocks(q, n, CHUNK), _to_blocks(k, n, CHUNK),
                                   _to_blocks(v, n, CHUNK), _to_blocks(log_f, n, CHUNK)))
    return _from_blocks(o)


def gated_delta_rule(q, k, v, log_alpha, beta):
    B, H, S, Dk = q.shape
    n = S // CHUNK
    q = q * Dk ** -0.5

    def chunks(t):
        return t.reshape((B, H, n, CHUNK) + t.shape[3:])

    qc, kc, vc, bc = chunks(q), chunks(k), chunks(v), chunks(beta)
    g = jnp.cumsum(chunks(log_alpha), axis=-1)
    incl = jnp.tril(jnp.ones((CHUNK, CHUNK), dtype=bool))
    strict = jnp.tril(jnp.ones((CHUNK, CHUNK), dtype=bool), -1)
    rel = g[..., :, None] - g[..., None, :]
    decay = jnp.where(incl, jnp.exp(jnp.where(incl, rel, 0.0)), 0.0)
    k_beta = kc * bc[..., None]
    kk = jnp.einsum('bhntd,bhnsd->bhnts', k_beta, kc) * decay
    lhs = jnp.eye(CHUNK, dtype=jnp.float32) + jnp.where(strict, kk, 0.0)
    u = lax.linalg.triangular_solve(lhs, vc * bc[..., None], left_side=True, lower=True, unit_diagonal=True)
    w = lax.linalg.triangular_solve(lhs, k_beta * jnp.exp(g)[..., None], left_side=True, lower=True, unit_diagonal=True)
    qk = jnp.einsum('bhntd,bhnsd->bhnts', qc, kc) * decay

    def step(state, inp):
        qi, ki, ui, wi, qki, gi = inp
        v_new = ui - jnp.einsum('bhtd,bhdv->bhtv', wi, state)
        o = (jnp.einsum('bhtd,bhdv->bhtv', qi * jnp.exp(gi)[..., None], state)
             + jnp.einsum('bhts,bhsv->bhtv', qki, v_new))
        g_last = gi[..., -1:]
        state = (jnp.exp(g_last)[..., None] * state
                 + jnp.einsum('bhsd,bhsv->bhdv', ki * jnp.exp(g_last - gi)[..., None], v_new))
        return state, o

    state0 = jnp.zeros((B, H, Dk, v.shape[-1]), jnp.float32)
    xs = tuple(jnp.moveaxis(t, 2, 0) for t in (qc, kc, u, w, qk, g))
    _, o = lax.scan(step, state0, xs)
    return _from_blocks(o)


def hybrid_mixer(h, w_in, w_out, out_gain, lower_bound, conv_w, a_log, dt_bias):
    f32 = jnp.float32
    proj = (h @ w_in).astype(f32)
    (a_q, a_k, a_v, b_q, b_k, b_v, c_q, c_f, c_i, c_g,
     d_qkv, d_z, d_a, d_b) = jnp.split(proj, _split_points(IN_SIZES), axis=-1)
    o_a = head_rmsnorm(merge_heads(dilated_attention(heads(a_q), heads(a_k), heads(a_v))), out_gain[0])
    o_b = head_rmsnorm(merge_heads(stick_breaking_attention(heads(b_q), heads(b_k), heads(b_v))), out_gain[1])
    lb = lower_bound.astype(f32)
    log_f = jnp.logaddexp(jnp.log(jnp.maximum(lb, LB_FLOOR)), jnp.log1p(-lb) + jax.nn.log_sigmoid(c_f))
    c_k = (1.0 - lb) * jax.nn.sigmoid(-c_f)
    o_c = hgrn2_recurrence(heads(jax.nn.silu(c_q)), heads(c_k), heads(c_i), heads(log_f))
    o_c = head_rmsnorm(merge_heads(o_c), out_gain[2]) * jax.nn.silu(c_g)
    d_qkv = jax.nn.silu(causal_depthwise_conv(d_qkv, conv_w.astype(f32)))
    d_q, d_k, d_v = jnp.split(d_qkv, 3, axis=-1)
    beta = jax.nn.sigmoid(d_b).transpose(0, 2, 1)
    log_alpha = (-jnp.exp(a_log.astype(f32)) * jax.nn.softplus(d_a + dt_bias.astype(f32))).transpose(0, 2, 1)
    o_d = gated_delta_rule(l2norm(heads(d_q)), l2norm(heads(d_k)), heads(d_v), log_alpha, beta)
    o_d = head_rmsnorm(merge_heads(o_d), out_gain[3]) * jax.nn.silu(d_z)
    mixed = jnp.concatenate([o_a, o_b, o_c, o_d], axis=-1).astype(h.dtype)
    return mixed @ w_out


def setup_inputs(seed: int = 0) -> dict:
    key = jax.random.key(seed)
    ks = jax.random.split(key, 16)
    nrm = jax.random.normal
    x = nrm(ks[0], (BATCH, SEQ, D_MODEL), jnp.float32)
    c = nrm(ks[1], (BATCH, D_MODEL), jnp.float32)
    w_mod = nrm(ks[2], (DEPTH, D_MODEL, N_MOD * D_MODEL), jnp.float32) * (0.5 * D_MODEL ** -0.5)
    b_mod = 0.02 * nrm(ks[3], (DEPTH, N_MOD * D_MODEL), jnp.float32)
    norm_gain = 1.0 + 0.05 * nrm(ks[4], (DEPTH, 6, D_MODEL), jnp.float32)
    w_in = nrm(ks[5], (DEPTH, D_MODEL, IN_COLS), jnp.float32) * D_MODEL ** -0.5
    w_out = nrm(ks[6], (DEPTH, D_MIX, D_MODEL), jnp.float32) * D_MIX ** -0.5
    mix_out_gain = 1.0 + 0.05 * nrm(ks[7], (DEPTH, N_GROUPS, GW), jnp.float32)
    hgrn_lb_logits = 0.5 * nrm(ks[8], (DEPTH, GW), jnp.float32)
    dn_conv_w = nrm(ks[9], (DEPTH, CONV_WIDTH, 3 * GW), jnp.float32) * CONV_WIDTH ** -0.5
    dn_a_log = jnp.log(jax.random.uniform(ks[10], (DEPTH, GH), jnp.float32, 1.0, 16.0))
    dt = jnp.exp(jax.random.uniform(ks[11], (DEPTH, GH), jnp.float32, math.log(1e-3), math.log(1e-1)))
    dn_dt_bias = dt + jnp.log(-jnp.expm1(-dt))
    ffn1_w13 = nrm(ks[12], (DEPTH, D_MODEL, 2 * D_FF), jnp.float32) * D_MODEL ** -0.5
    ffn1_w2 = nrm(ks[13], (DEPTH, D_FF, D_MODEL), jnp.float32) * D_FF ** -0.5
    ffn2_w13 = nrm(ks[14], (DEPTH, D_MODEL, 2 * D_FF), jnp.float32) * D_MODEL ** -0.5
    ffn2_w2 = nrm(ks[15], (DEPTH, D_FF, D_MODEL), jnp.float32) * D_FF ** -0.5
    return {'x': x, 'c': c, 'w_mod': w_mod, 'b_mod': b_mod, 'norm_gain': norm_gain,
            'w_in': w_in, 'w_out': w_out, 'mix_out_gain': mix_out_gain,
            'hgrn_lb_logits': hgrn_lb_logits, 'dn_conv_w': dn_conv_w, 'dn_a_log': dn_a_log,
            'dn_dt_bias': dn_dt_bias, 'ffn1_w13': ffn1_w13, 'ffn1_w2': ffn1_w2,
            'ffn2_w13': ffn2_w13, 'ffn2_w2': ffn2_w2}


def reference(x, c, w_mod, b_mod, norm_gain, w_in, w_out, mix_out_gain, hgrn_lb_logits,
              dn_conv_w, dn_a_log, dn_dt_bias, ffn1_w13, ffn1_w2, ffn2_w13, ffn2_w2):
    lb_p = jax.nn.softmax(hgrn_lb_logits.astype(jnp.float32), axis=0)
    lower_bounds = jnp.cumsum(lb_p, axis=0) - lb_p[0]
    cond = jax.nn.silu(c)
    for l in range(DEPTH):
        mod = (cond @ w_mod[l] + b_mod[l])[:, None, :]
        sh1, sc1, g1, sh2, sc2, g2, sh3, sc3, g3 = jnp.split(mod, N_MOD, axis=-1)
        h = modulate(rmsnorm(x, norm_gain[l, 0]), sh1, sc1)
        x = x + (0.5 * g1 * rmsnorm(swiglu(h, ffn1_w13[l], ffn1_w2[l]), norm_gain[l, 1])).astype(x.dtype)
        h = modulate(rmsnorm(x, norm_gain[l, 2]), sh2, sc2)
        y = hybrid_mixer(h, w_in[l], w_out[l], mix_out_gain[l], lower_bounds[l],
                         dn_conv_w[l], dn_a_log[l], dn_dt_bias[l])
        x = x + (g2 * rmsnorm(y, norm_gain[l, 3])).astype(x.dtype)
        h = modulate(rmsnorm(x, norm_gain[l, 4]), sh3, sc3)
        x = x + (0.5 * g3 * rmsnorm(swiglu(h, ffn2_w13[l], ffn2_w2[l]), norm_gain[l, 5])).astype(x.dtype)
    return x
```

```python
import functools

import jax
import jax.numpy as jnp
import numpy as np
from jax import lax
from jax.experimental import pallas as pl
from jax.experimental.pallas import tpu as pltpu

F32 = jnp.float32
BF16 = jnp.bfloat16

HEAD_DIM = 128
N_GROUPS = 4
N_MOD = 9
CONV_WIDTH = 4
DILATED_BRANCHES = ((128, 1), (512, 4), (2048, 16))
EPS = 1e-6
NEG_BIG = -1e30
LB_FLOOR = 1e-30

HGRN_CHUNK = 64
HGRN_SUB = 8
GDN_CHUNK = 128
GDN_INV_BASE = 8
CONV_HALO = 8
VMEM_LIMIT = 56 * 1024 * 1024

_NT = (((1,), (1,)), ((), ()))
_TN = (((0,), (0,)), ((), ()))


def _dot(a, b):
    return jnp.dot(a, b, preferred_element_type=F32)


def _dot_nt(a, b):
    return lax.dot_general(a, b, _NT, preferred_element_type=F32)


def _dot_tn(a, b):
    return lax.dot_general(a, b, _TN, preferred_element_type=F32)


def _split3(a):
    hi = a.astype(BF16)
    r = a - hi.astype(F32)
    mid = r.astype(BF16)
    lo = (r - mid.astype(F32)).astype(BF16)
    return hi, mid, lo


def _dot_exact_lhs(a_bf16, b):
    hi, mid, lo = _split3(b)
    return _dot(a_bf16, hi) + (_dot(a_bf16, mid) + _dot(a_bf16, lo))


def _dot_hp(a, b):
    a_hi = a.astype(BF16)
    a_lo = (a - a_hi.astype(F32)).astype(BF16)
    b_hi = b.astype(BF16)
    b_lo = (b - b_hi.astype(F32)).astype(BF16)
    return _dot(a_hi, b_hi) + (_dot(a_hi, b_lo) + _dot(a_lo, b_hi))


def _sigmoid(x):
    return 1.0 / (1.0 + jnp.exp(-x))


def _silu(x):
    return x * _sigmoid(x)


def _softplus(x):
    return jnp.maximum(x, 0.0) + jnp.log1p(jnp.exp(-jnp.abs(x)))


def _log_sigmoid(x):
    return jnp.minimum(x, 0.0) - jnp.log1p(jnp.exp(-jnp.abs(x)))


def _rms(x, gain):
    return x * lax.rsqrt(jnp.mean(x * x, axis=-1, keepdims=True) + EPS) * gain


def _mod_kernel(c_ref, w_ref, b_ref, o_ref):
    cond = _silu(c_ref[...]).astype(BF16)
    o_ref[...] = _dot(cond, w_ref[...].astype(BF16)) + b_ref[...]


def _modulation(c_pad, w_mod, b_mod, tn):
    depth, d, n = w_mod.shape
    rows = c_pad.shape[0]
    return pl.pallas_call(
        _mod_kernel,
        grid=(depth, n // tn),
        in_specs=[
            pl.BlockSpec((rows, d), lambda l, j: (0, 0)),
            pl.BlockSpec((None, d, tn), lambda l, j: (l, 0, j)),
            pl.BlockSpec((None, 1, tn), lambda l, j: (l, 0, j)),
        ],
        out_specs=pl.BlockSpec((None, rows, tn), lambda l, j: (l, 0, j)),
        out_shape=jax.ShapeDtypeStruct((depth, rows, n), F32),
        compiler_params=pltpu.CompilerParams(
            dimension_semantics=("parallel", "parallel"), vmem_limit_bytes=VMEM_LIMIT),
        name="modulation",
    )(c_pad, w_mod, b_mod.reshape(depth, 1, n))


def _ffn_kernel(x_ref, sh_ref, sc_ref, gt_ref, ga_ref, gb_ref, w1_ref, w3_ref, w2_ref,
                o_ref, h_ref, acc_ref):
    j = pl.program_id(1)

    @pl.when(j == 0)
    def _():
        h = _rms(x_ref[...], ga_ref[...]) * (1.0 + sc_ref[...]) + sh_ref[...]
        h_ref[...] = h.astype(BF16)
        acc_ref[...] = jnp.zeros_like(acc_ref)

    h = h_ref[...]
    gate = _dot(h, w1_ref[...])
    up = _dot(h, w3_ref[...])
    act = (_silu(gate) * up).astype(BF16)
    acc_ref[...] += _dot(act, w2_ref[...])

    @pl.when(j == pl.num_programs(1) - 1)
    def _():
        o_ref[...] = x_ref[...] + (0.5 * gt_ref[...]) * _rms(acc_ref[...], gb_ref[...])


def _ffn(x, shift, scale, gate, gain_a, gain_b, w13, w2, layer, tiles_per_batch, tm, tf):
    n, d = x.shape
    dff = w2.shape[1]
    nf = dff // tf
    row = lambda i, j: (i, 0)
    per_batch = lambda i, j: (i // tiles_per_batch, 0, 0)
    const = lambda i, j: (0, 0)
    return pl.pallas_call(
        _ffn_kernel,
        grid=(n // tm, nf),
        in_specs=[
            pl.BlockSpec((tm, d), row),
            pl.BlockSpec((None, 1, d), per_batch),
            pl.BlockSpec((None, 1, d), per_batch),
            pl.BlockSpec((None, 1, d), per_batch),
            pl.BlockSpec((1, d), const),
            pl.BlockSpec((1, d), const),
            pl.BlockSpec((None, d, tf), lambda i, j: (layer, 0, j)),
            pl.BlockSpec((None, d, tf), lambda i, j: (layer, 0, nf + j)),
            pl.BlockSpec((None, tf, d), lambda i, j: (layer, j, 0)),
        ],
        out_specs=pl.BlockSpec((tm, d), row),
        out_shape=jax.ShapeDtypeStruct((n, d), F32),
        scratch_shapes=[pltpu.VMEM((tm, d), BF16), pltpu.VMEM((tm, d), F32)],
        compiler_params=pltpu.CompilerParams(
            dimension_semantics=("parallel", "arbitrary"), vmem_limit_bytes=VMEM_LIMIT),
        name="ffn",
    )(x, shift, scale, gate, gain_a, gain_b, w13, w13, w2)


def _inproj_kernel(x_ref, sh_ref, sc_ref, ga_ref, w_ref, o_ref, h_ref):
    @pl.when(pl.program_id(1) == 0)
    def _():
        h = _rms(x_ref[...], ga_ref[...]) * (1.0 + sc_ref[...]) + sh_ref[...]
        h_ref[...] = h.astype(BF16)

    o_ref[...] = _dot(h_ref[...], w_ref[...])


def _inproj(x, shift, scale, gain, w_in, layer, tiles_per_batch, tm, tn):
    n, d = x.shape
    cols = w_in.shape[2]
    per_batch = lambda i, j: (i // tiles_per_batch, 0, 0)
    return pl.pallas_call(
        _inproj_kernel,
        grid=(n // tm, cols // tn),
        in_specs=[
            pl.BlockSpec((tm, d), lambda i, j: (i, 0)),
            pl.BlockSpec((None, 1, d), per_batch),
            pl.BlockSpec((None, 1, d), per_batch),
            pl.BlockSpec((1, d), lambda i, j: (0, 0)),
            pl.BlockSpec((None, d, tn), lambda i, j: (layer, 0, j)),
        ],
        out_specs=pl.BlockSpec((tm, tn), lambda i, j: (i, j)),
        out_shape=jax.ShapeDtypeStruct((n, cols), F32),
        scratch_shapes=[pltpu.VMEM((tm, d), BF16)],
        compiler_params=pltpu.CompilerParams(
            dimension_semantics=("parallel", "arbitrary"), vmem_limit_bytes=VMEM_LIMIT),
        name="inproj",
    )(x, shift, scale, gain, w_in)


def _dilated_counts(tq, tk):
    reach = max(w for w, _ in DILATED_BRANCHES)
    nd = (reach + tq - 1) // tk + 1
    d = np.arange(nd)[:, None, None]
    i = np.arange(tq)[None, :, None]
    j = np.arange(tk)[None, None, :]
    delta = d * tk + i - j
    cnt = np.zeros(delta.shape, np.float32)
    for window, dil in DILATED_BRANCHES:
        cnt += ((delta >= 0) & (delta <= window) & (delta % dil == 0)).astype(np.float32)
    return cnt


def _attn_a_kernel(q_ref, k_ref, v_ref, cnt_ref, gain_ref, o_ref, kb_ref, vb_ref, *, tq, tk, nd):
    qi = pl.program_id(2)

    @pl.when(qi == 0)
    def _():
        kb_ref[...] = k_ref[...].astype(BF16)
        vb_ref[...] = v_ref[...].astype(BF16)

    q = q_ref[...].astype(BF16)
    scale = HEAD_DIM ** -0.5
    last = (qi * tq + tq) // tk - 1

    def body(c, carry):
        m, l, acc = carry
        off = pl.multiple_of(c * tk, tk)
        k = kb_ref[pl.ds(off, tk), :]
        v = vb_ref[pl.ds(off, tk), :]
        s = _dot_nt(q, k) * scale
        cnt = cnt_ref[last - c].astype(F32)
        sm = jnp.where(cnt > 0.0, s, NEG_BIG)
        m_new = jnp.maximum(m, jnp.max(sm, axis=-1, keepdims=True))
        alpha = jnp.exp(m - m_new)
        p = jnp.exp(sm - m_new) * cnt
        l = alpha * l + jnp.sum(p, axis=-1, keepdims=True)
        acc = alpha * acc + _dot(p.astype(BF16), v)
        return m_new, l, acc

    init = (jnp.full((tq, 1), NEG_BIG, F32), jnp.zeros((tq, 1), F32),
            jnp.zeros((tq, HEAD_DIM), F32))
    m, l, acc = lax.fori_loop(jnp.maximum(last - (nd - 1), 0), last + 1, body, init)
    o_ref[...] = _rms(acc / l, gain_ref[...]).astype(BF16)


def _attn_a(proj, gains, layer, batch, seq, gh, col0, tq, tk):
    cnt = jnp.asarray(_dilated_counts(tq, tk), BF16)
    nd = cnt.shape[0]
    nq = seq // tq
    return pl.pallas_call(
        functools.partial(_attn_a_kernel, tq=tq, tk=tk, nd=nd),
        grid=(batch, gh, nq),
        in_specs=[
            pl.BlockSpec((tq, HEAD_DIM), lambda b, h, i: (b * nq + i, col0 + h)),
            pl.BlockSpec((seq, HEAD_DIM), lambda b, h, i: (b, col0 + gh + h)),
            pl.BlockSpec((seq, HEAD_DIM), lambda b, h, i: (b, col0 + 2 * gh + h)),
            pl.BlockSpec((nd, tq, tk), lambda b, h, i: (0, 0, 0)),
            pl.BlockSpec((None, None, 1, HEAD_DIM), lambda b, h, i: (layer, h, 0, 0)),
        ],
        out_specs=pl.BlockSpec((tq, HEAD_DIM), lambda b, h, i: (b * nq + i, h)),
        out_shape=jax.ShapeDtypeStruct((batch * seq, gh * HEAD_DIM), BF16),
        scratch_shapes=[pltpu.VMEM((seq, HEAD_DIM), BF16), pltpu.VMEM((seq, HEAD_DIM), BF16)],
        compiler_params=pltpu.CompilerParams(
            dimension_semantics=("parallel", "parallel", "arbitrary"),
            vmem_limit_bytes=VMEM_LIMIT),
        name="dilated_attention",
    )(proj, proj, proj, cnt, gains)


def _attn_b_kernel(q_ref, k_ref, v_ref, gain_ref, o_ref, kb_ref, vb_ref, *, tq, tk):
    qi = pl.program_id(2)

    @pl.when(qi == 0)
    def _():
        kb_ref[...] = k_ref[...].astype(BF16)
        vb_ref[...] = v_ref[...].astype(BF16)

    q = q_ref[...].astype(BF16)
    scale = HEAD_DIM ** -0.5
    jj = lax.broadcasted_iota(jnp.int32, (tk, tk), 0)
    ss = lax.broadcasted_iota(jnp.int32, (tk, tk), 1)
    suffix = jnp.where(jj >= ss, 1.0, 0.0).astype(BF16)
    n_diag = tq // tk
    first_diag = qi * n_diag

    def step(c, carry, masked):
        tail, acc = carry
        off = pl.multiple_of(c * tk, tk)
        k = kb_ref[pl.ds(off, tk), :]
        v = vb_ref[pl.ds(off, tk), :]
        z = _dot_nt(q, k) * scale
        log_keep = _log_sigmoid(-z)
        if masked:
            t_pos = qi * tq + lax.broadcasted_iota(jnp.int32, (tq, tk), 0)
            s_pos = c * tk + lax.broadcasted_iota(jnp.int32, (tq, tk), 1)
            causal = s_pos < t_pos
            log_keep = jnp.where(causal, log_keep, 0.0)
        lk_hi = log_keep.astype(BF16)
        lk_lo = (log_keep - lk_hi.astype(F32)).astype(BF16)
        within = _dot(lk_hi, suffix) + _dot(lk_lo, suffix)
        a = jnp.exp(z + (within + tail))
        if masked:
            a = jnp.where(causal, a, 0.0)
        acc = acc + _dot(a.astype(BF16), v)
        tail = tail + jnp.sum(log_keep, axis=-1, keepdims=True)
        return tail, acc

    carry = (jnp.zeros((tq, 1), F32), jnp.zeros((tq, HEAD_DIM), F32))
    for d in range(n_diag - 1, -1, -1):
        carry = step(first_diag + d, carry, True)
    carry = lax.fori_loop(
        0, first_diag, lambda i, cr: step(first_diag - 1 - i, cr, False), carry)
    o_ref[...] = _rms(carry[1], gain_ref[...]).astype(BF16)


def _attn_b(proj, gains, layer, batch, seq, gh, col0, tq, tk):
    nq = seq // tq
    return pl.pallas_call(
        functools.partial(_attn_b_kernel, tq=tq, tk=tk),
        grid=(batch, gh, nq),
        in_specs=[
            pl.BlockSpec((tq, HEAD_DIM), lambda b, h, i: (b * nq + i, col0 + h)),
            pl.BlockSpec((seq, HEAD_DIM), lambda b, h, i: (b, col0 + gh + h)),
            pl.BlockSpec((seq, HEAD_DIM), lambda b, h, i: (b, col0 + 2 * gh + h)),
            pl.BlockSpec((None, None, 1, HEAD_DIM), lambda b, h, i: (layer, gh + h, 0, 0)),
        ],
        out_specs=pl.BlockSpec((tq, HEAD_DIM), lambda b, h, i: (b * nq + i, h)),
        out_shape=jax.ShapeDtypeStruct((batch * seq, gh * HEAD_DIM), BF16),
        scratch_shapes=[pltpu.VMEM((seq, HEAD_DIM), BF16), pltpu.VMEM((seq, HEAD_DIM), BF16)],
        compiler_params=pltpu.CompilerParams(
            dimension_semantics=("parallel", "parallel", "arbitrary"),
            vmem_limit_bytes=VMEM_LIMIT),
        name="stick_breaking_attention",
    )(proj, proj, proj, gains)


def _hgrn_chunk(q, k, v, b, state_t):
    c = HGRN_CHUNK
    nsub = c // HGRN_SUB
    o_inter = _dot_nt((q * jnp.exp(b)).astype(BF16), state_t.astype(BF16))
    row_id = lax.broadcasted_iota(jnp.int32, (HGRN_SUB, 1), 0)
    blocks = [o_inter[HGRN_SUB * i:HGRN_SUB * (i + 1)] for i in range(nsub)]
    for sub in range(nsub):
        lo = sub * HGRN_SUB
        q_t = q[lo:]
        b_t = b[lo:]
        contrib = None
        for si in range(HGRN_SUB):
            s = lo + si
            decay = jnp.exp(jnp.minimum(b_t - b[s:s + 1], 0.0))
            score = jnp.sum(q_t * decay * k[s:s + 1], axis=-1, keepdims=True)
            if si > 0:
                head = jnp.where(row_id >= si, score[:HGRN_SUB], 0.0)
                score = jnp.concatenate([head, score[HGRN_SUB:]], axis=0) if sub < nsub - 1 else head
            term = score * v[s:s + 1]
            contrib = term if contrib is None else contrib + term
        for i in range(sub, nsub):
            r = (i - sub) * HGRN_SUB
            blocks[i] = blocks[i] + contrib[r:r + HGRN_SUB]
    o = jnp.concatenate(blocks, axis=0)
    b_last = b[c - 1:c]
    k_decayed = k * jnp.exp(b_last - b)
    new_state_t = jnp.exp(b_last) * state_t + _dot_tn(v.astype(BF16), k_decayed.astype(BF16))
    return o, new_state_t


def _hgrn_kernel(cq_ref, cf_ref, ci_ref, cg_ref, lb_ref, gain_ref, o_ref, state_ref, of_ref, *, gh):
    ts = cq_ref.shape[0]
    c = HGRN_CHUNK

    @pl.when(pl.program_id(1) == 0)
    def _():
        state_ref[...] = jnp.zeros_like(state_ref)

    lb = lb_ref[...]
    log_lb = jnp.log(jnp.maximum(lb, LB_FLOOR))
    log_1m_lb = jnp.log1p(-lb)
    ii = lax.broadcasted_iota(jnp.int32, (c, c), 0)
    jj = lax.broadcasted_iota(jnp.int32, (c, c), 1)
    prefix = jnp.where(jj <= ii, 1.0, 0.0).astype(BF16)

    def chunk(ci, _):
        base = pl.multiple_of(ci * c, c)
        rows = pl.ds(base, c)
        cf = cf_ref[rows, :]
        x = log_1m_lb + _log_sigmoid(cf)
        hi = jnp.maximum(log_lb, x)
        log_f = hi + jnp.log1p(jnp.exp(-jnp.abs(log_lb - x)))
        b_all = _dot_exact_lhs(prefix, log_f)
        k_all = (1.0 - lb) * _sigmoid(-cf)
        q_all = _silu(cq_ref[rows, :])
        v_all = ci_ref[rows, :]
        for h in range(gh):
            cols = slice(h * HEAD_DIM, (h + 1) * HEAD_DIM)
            o, st = _hgrn_chunk(q_all[:, cols], k_all[:, cols], v_all[:, cols], b_all[:, cols],
                                state_ref[h])
            state_ref[h] = st
            of_ref[rows, cols] = o
        return 0

    lax.fori_loop(0, ts // c, chunk, 0)

    for h in range(gh):
        cols = slice(h * HEAD_DIM, (h + 1) * HEAD_DIM)
        o = _rms(of_ref[:, cols], gain_ref[:, cols]) * _silu(cg_ref[:, cols])
        o_ref[:, cols] = o.astype(BF16)


def _hgrn(proj, lower_bound, gains, layer, batch, seq, gh, col0, ts):
    gw = gh * HEAD_DIM
    nt = seq // ts
    blk = lambda k: pl.BlockSpec((ts, gw), lambda b, i, k=k: (b * nt + i, col0 + k))
    return pl.pallas_call(
        functools.partial(_hgrn_kernel, gh=gh),
        grid=(batch, nt),
        in_specs=[
            blk(0), blk(1), blk(2), blk(3),
            pl.BlockSpec((None, 1, gw), lambda b, i: (layer, 0, 0)),
            pl.BlockSpec((None, None, 1, gw), lambda b, i: (layer, 2, 0, 0)),
        ],
        out_specs=pl.BlockSpec((ts, gw), lambda b, i: (b * nt + i, 0)),
        out_shape=jax.ShapeDtypeStruct((batch * seq, gw), BF16),
        scratch_shapes=[pltpu.VMEM((gh, HEAD_DIM, HEAD_DIM), F32), pltpu.VMEM((ts, gw), F32)],
        compiler_params=pltpu.CompilerParams(
            dimension_semantics=("parallel", "arbitrary"), vmem_limit_bytes=VMEM_LIMIT),
        name="hgrn2",
    )(proj, proj, proj, proj, lower_bound, gains)


def _gdn_kernel(xq_ref, xk_ref, xv_ref, z_ref, ab_ref, wq_ref, wk_ref, wv_ref, alog_ref, dt_ref,
                gain_ref, o_ref, pad_ref, act_ref, state_ref, of_ref, *, gh):
    ts = xq_ref.shape[0]
    c = GDN_CHUNK
    first = pl.program_id(1) == 0

    @pl.when(first)
    def _():
        state_ref[...] = jnp.zeros_like(state_ref)
        pad_ref[:, 0:CONV_HALO, :] = jnp.zeros((3, CONV_HALO, pad_ref.shape[2]), F32)

    @pl.when(jnp.logical_not(first))
    def _():
        pad_ref[:, 0:CONV_HALO, :] = pad_ref[:, ts:ts + CONV_HALO, :]

    for part, (x_ref, w_ref) in enumerate(((xq_ref, wq_ref), (xk_ref, wk_ref), (xv_ref, wv_ref))):
        pad_ref[part, CONV_HALO:CONV_HALO + ts, :] = x_ref[...]
        acc = None
        for tap in range(CONV_WIDTH):
            start = CONV_HALO - (CONV_WIDTH - 1) + tap
            term = w_ref[tap:tap + 1, :] * pad_ref[part, start:start + ts, :]
            acc = term if acc is None else acc + term
        act_ref[part] = _silu(acc)

    ab = ab_ref[:, 0:HEAD_DIM]
    log_alpha = -jnp.exp(alog_ref[...]) * _softplus(ab + dt_ref[...])
    beta_all = _sigmoid(ab)

    ii = lax.broadcasted_iota(jnp.int32, (c, c), 0)
    jj = lax.broadcasted_iota(jnp.int32, (c, c), 1)
    incl = jj <= ii
    strict = jj < ii
    prefix = jnp.where(incl, 1.0, 0.0).astype(BF16)
    eye = jnp.where(ii == jj, 1.0, 0.0)
    base_shift = int(np.log2(GDN_INV_BASE))
    in_base = jnp.where((ii >> base_shift) == (jj >> base_shift), 1.0, 0.0)
    merge_masks = []
    for shift in range(base_shift, int(np.log2(c))):
        same_pair = (ii >> (shift + 1)) == (jj >> (shift + 1))
        other_half = (ii >> shift) != (jj >> shift)
        merge_masks.append(jnp.where(same_pair & other_half, 1.0, 0.0))

    def unit_lower_inverse(a):
        d0 = a * in_base
        inv = eye - d0
        power = d0
        for _ in range(base_shift - 1):
            power = _dot_hp(power, power)
            inv = inv + _dot_hp(inv, power)
        for mask in merge_masks:
            inv = inv - _dot_hp(_dot_hp(inv, a * mask), inv)
        return inv

    for ci in range(ts // c):
        rows = slice(ci * c, (ci + 1) * c)
        g_all = _dot_exact_lhs(prefix, log_alpha[rows])
        g_rows = g_all.T
        for h in range(gh):
            cols = slice(h * HEAD_DIM, (h + 1) * HEAD_DIM)
            q = act_ref[0, rows, cols]
            k = act_ref[1, rows, cols]
            v = act_ref[2, rows, cols]
            q = q * lax.rsqrt(jnp.sum(q * q, axis=-1, keepdims=True) + EPS) * (HEAD_DIM ** -0.5)
            k = k * lax.rsqrt(jnp.sum(k * k, axis=-1, keepdims=True) + EPS)
            beta = beta_all[rows, gh + h:gh + h + 1]
            g_col = g_all[:, h:h + 1]
            g_row = g_rows[h:h + 1, :]
            g_last = g_all[c - 1:c, h:h + 1]
            decay = jnp.where(incl, jnp.exp(jnp.minimum(g_col - g_row, 0.0)), 0.0)
            k_beta = k * beta
            k16 = k.astype(BF16)
            kk = _dot_nt(k_beta.astype(BF16), k16) * decay
            a = jnp.where(strict, kk, 0.0)
            inv = unit_lower_inverse(a)
            u = _dot_hp(inv, v * beta)
            w = _dot_hp(inv, k_beta * jnp.exp(g_col))
            qk = _dot_nt(q.astype(BF16), k16) * decay
            state = state_ref[h]
            s16 = state.astype(BF16)
            v_new = u - _dot(w.astype(BF16), s16)
            vn16 = v_new.astype(BF16)
            o = _dot((q * jnp.exp(g_col)).astype(BF16), s16) + _dot(qk.astype(BF16), vn16)
            k_decayed = k * jnp.exp(g_last - g_col)
            state_ref[h] = jnp.exp(g_last) * state + _dot_tn(k_decayed.astype(BF16), vn16)
            of_ref[rows, cols] = o

    for h in range(gh):
        cols = slice(h * HEAD_DIM, (h + 1) * HEAD_DIM)
        o = _rms(of_ref[:, cols], gain_ref[:, cols]) * _silu(z_ref[:, cols])
        o_ref[:, cols] = o.astype(BF16)


def _gdn(proj, conv_w, alog_row, dt_row, gains, layer, batch, seq, gh, col0, ts):
    gw = gh * HEAD_DIM
    nt = seq // ts
    blk = lambda k: pl.BlockSpec((ts, gw), lambda b, i, k=k: (b * nt + i, col0 + k))
    wblk = lambda k: pl.BlockSpec((None, CONV_WIDTH, gw), lambda b, i, k=k: (layer, 0, k))
    row = pl.BlockSpec((None, 1, HEAD_DIM), lambda b, i: (layer, 0, 0))
    return pl.pallas_call(
        functools.partial(_gdn_kernel, gh=gh),
        grid=(batch, nt),
        in_specs=[
            blk(0), blk(1), blk(2), blk(3), blk(4),
            wblk(0), wblk(1), wblk(2), row, row,
            pl.BlockSpec((None, None, 1, gw), lambda b, i: (layer, 3, 0, 0)),
        ],
        out_specs=pl.BlockSpec((ts, gw), lambda b, i: (b * nt + i, 0)),
        out_shape=jax.ShapeDtypeStruct((batch * seq, gw), BF16),
        scratch_shapes=[
            pltpu.VMEM((3, ts + CONV_HALO, gw), F32),
            pltpu.VMEM((3, ts, gw), F32),
            pltpu.VMEM((gh, HEAD_DIM, HEAD_DIM), F32),
            pltpu.VMEM((ts, gw), F32),
        ],
        compiler_params=pltpu.CompilerParams(
            dimension_semantics=("parallel", "arbitrary"), vmem_limit_bytes=VMEM_LIMIT),
        name="gated_deltanet",
    )(proj, proj, proj, proj, proj, conv_w, conv_w, conv_w, alog_row, dt_row, gains)


def _outproj_kernel(x_ref, oa_ref, ob_ref, oc_ref, od_ref, w_ref, gt_ref, gain_ref, o_ref):
    gw = oa_ref.shape[1]
    y = None
    for g, m_ref in enumerate((oa_ref, ob_ref, oc_ref, od_ref)):
        part = _dot(m_ref[...], w_ref[g * gw:(g + 1) * gw, :])
        y = part if y is None else y + part
    o_ref[...] = x_ref[...] + gt_ref[...] * _rms(y, gain_ref[...])


def _outproj(x, groups, w_out, gate, gain, layer, tiles_per_batch, tm):
    n, d = x.shape
    gw = groups[0].shape[1]
    gspec = pl.BlockSpec((tm, gw), lambda i: (i, 0))
    return pl.pallas_call(
        _outproj_kernel,
        grid=(n // tm,),
        in_specs=[
            pl.BlockSpec((tm, d), lambda i: (i, 0)),
            gspec, gspec, gspec, gspec,
            pl.BlockSpec((None, N_GROUPS * gw, d), lambda i: (layer, 0, 0)),
            pl.BlockSpec((None, 1, d), lambda i: (i // tiles_per_batch, 0, 0)),
            pl.BlockSpec((1, d), lambda i: (0, 0)),
        ],
        out_specs=pl.BlockSpec((tm, d), lambda i: (i, 0)),
        out_shape=jax.ShapeDtypeStruct((n, d), F32),
        compiler_params=pltpu.CompilerParams(
            dimension_semantics=("parallel",), vmem_limit_bytes=VMEM_LIMIT),
        name="outproj",
    )(x, *groups, w_out, gate, gain)


def _largest_tile(n, cap):
    t = min(n, cap)
    while n % t:
        t //= 2
    return t


def kernel(x, c, w_mod, b_mod, norm_gain, w_in, w_out, mix_out_gain, hgrn_lb_logits, dn_conv_w,
           dn_a_log, dn_dt_bias, ffn1_w13, ffn1_w2, ffn2_w13, ffn2_w2):
    batch, seq, d = x.shape
    depth = w_mod.shape[0]
    gw = d // N_GROUPS
    gh = gw // HEAD_DIM
    dff = ffn1_w2.shape[1]
    n = batch * seq

    tm_ffn = _largest_tile(seq, 512)
    tf = dff // 11 if dff % 11 == 0 and (dff // 11) % 128 == 0 else _largest_tile(dff, 512)
    tm_proj = _largest_tile(seq, 1024)
    tm_out = _largest_tile(seq, 512)
    tq = _largest_tile(seq, 256)
    ts_c = _largest_tile(seq, 1024)
    ts_d = _largest_tile(seq, 512)

    in_cols = w_in.shape[2]
    pad_cols = (-in_cols) % gw
    w_in16 = jnp.pad(w_in, ((0, 0), (0, 0), (0, pad_cols))).astype(BF16)
    w_out16 = w_out.astype(BF16)
    ffn_w = [(ffn1_w13.astype(BF16), ffn1_w2.astype(BF16)),
             (ffn2_w13.astype(BF16), ffn2_w2.astype(BF16))]
    lb_p = jax.nn.softmax(hgrn_lb_logits.astype(F32), axis=0)
    lower_bounds = (jnp.cumsum(lb_p, axis=0) - lb_p[0]).reshape(depth, 1, gw)
    gains = mix_out_gain.reshape(depth, N_GROUPS, 1, gw)
    head_gains = mix_out_gain.reshape(depth, N_GROUPS * gh, 1, HEAD_DIM)
    lane_pad = ((0, 0), (0, HEAD_DIM - gh))
    alog_row = jnp.pad(dn_a_log.astype(F32), lane_pad).reshape(depth, 1, HEAD_DIM)
    dt_row = jnp.pad(dn_dt_bias.astype(F32), lane_pad).reshape(depth, 1, HEAD_DIM)
    conv_w = dn_conv_w.astype(F32)

    c_rows = -(-batch // 8) * 8
    c_pad = jnp.pad(c, ((0, c_rows - batch), (0, 0)))
    mod = _modulation(c_pad, w_mod, b_mod, _largest_tile(N_MOD * d, 1024))[:, :batch]
    mod = mod.reshape(depth, batch, N_MOD, 1, d)

    xf = x.reshape(n, d)
    for l in range(depth):
        sh1, sc1, g1, sh2, sc2, g2, sh3, sc3, g3 = (mod[l, :, i] for i in range(N_MOD))
        ng = norm_gain[l].reshape(6, 1, d)
        xf = _ffn(xf, sh1, sc1, g1, ng[0], ng[1], *ffn_w[0], l, seq // tm_ffn, tm_ffn, tf)
        proj = _inproj(xf, sh2, sc2, ng[2], w_in16, l, seq // tm_proj, tm_proj, gw)
        hb = gw // HEAD_DIM
        o_a = _attn_a(proj, head_gains, l, batch, seq, gh, 0 * hb, tq, tq)
        o_b = _attn_b(proj, head_gains, l, batch, seq, gh, 3 * hb, tq, HEAD_DIM)
        o_c = _hgrn(proj, lower_bounds, gains, l, batch, seq, gh, 6, ts_c)
        o_d = _gdn(proj, conv_w, alog_row, dt_row, gains, l, batch, seq, gh, 10, ts_d)
        xf = _outproj(xf, (o_a, o_b, o_c, o_d), w_out16, g2, ng[3], l, seq // tm_out, tm_out)
        xf = _ffn(xf, sh3, sc3, g3, ng[4], ng[5], *ffn_w[1], l, seq // tm_ffn, tm_ffn, tf)
    return xf.reshape(batch, seq, d)
```

```python
import functools

import jax
import jax.numpy as jnp
import numpy as np
from jax import lax
from jax.experimental import pallas as pl
from jax.experimental.pallas import tpu as pltpu

F32 = jnp.float32
BF16 = jnp.bfloat16

HEAD_DIM = 128
N_GROUPS = 4
N_MOD = 9
CONV_WIDTH = 4
DILATED_BRANCHES = ((128, 1), (512, 4), (2048, 16))
EPS = 1e-6
NEG_BIG = -1e30
LB_FLOOR = 1e-30

HGRN_CHUNK = 32
HGRN_SUB = 8
GDN_CHUNK = 128
GDN_INV_BASE = 8
CONV_HALO = 8
VMEM_LIMIT = 56 * 1024 * 1024

_NT = (((1,), (1,)), ((), ()))
_TN = (((0,), (0,)), ((), ()))


def _dot(a, b):
    return jnp.dot(a, b, preferred_element_type=F32)


def _dot_nt(a, b):
    return lax.dot_general(a, b, _NT, preferred_element_type=F32)


def _dot_tn(a, b):
    return lax.dot_general(a, b, _TN, preferred_element_type=F32)


def _split3(a):
    hi = a.astype(BF16)
    r = a - hi.astype(F32)
    mid = r.astype(BF16)
    lo = (r - mid.astype(F32)).astype(BF16)
    return hi, mid, lo


def _dot_exact_lhs(a_bf16, b):
    hi, mid, lo = _split3(b)
    return _dot(a_bf16, hi) + (_dot(a_bf16, mid) + _dot(a_bf16, lo))


def _dot16(a, b):
    return _dot(a.astype(BF16), b.astype(BF16))


def _sigmoid(x):
    return 1.0 / (1.0 + jnp.exp(-x))


def _silu(x):
    return x * _sigmoid(x)


def _softplus(x):
    return jnp.maximum(x, 0.0) + jnp.log1p(jnp.exp(-jnp.abs(x)))


def _log_sigmoid(x):
    return jnp.minimum(x, 0.0) - jnp.log1p(jnp.exp(-jnp.abs(x)))


def _rms(x, gain):
    return x * lax.rsqrt(jnp.mean(x * x, axis=-1, keepdims=True) + EPS) * gain


def _mod_kernel(c_ref, w_ref, b_ref, o_ref):
    cond = _silu(c_ref[...]).astype(BF16)
    o_ref[...] = _dot(cond, w_ref[...].astype(BF16)) + b_ref[...]


def _modulation(c_pad, w_mod, b_mod, tn):
    depth, d, n = w_mod.shape
    rows = c_pad.shape[0]
    return pl.pallas_call(
        _mod_kernel,
        grid=(depth, n // tn),
        in_specs=[
            pl.BlockSpec((rows, d), lambda l, j: (0, 0)),
            pl.BlockSpec((None, d, tn), lambda l, j: (l, 0, j)),
            pl.BlockSpec((None, 1, tn), lambda l, j: (l, 0, j)),
        ],
        out_specs=pl.BlockSpec((None, rows, tn), lambda l, j: (l, 0, j)),
        out_shape=jax.ShapeDtypeStruct((depth, rows, n), F32),
        compiler_params=pltpu.CompilerParams(
            dimension_semantics=("parallel", "parallel"), vmem_limit_bytes=VMEM_LIMIT),
        name="modulation",
    )(c_pad, w_mod, b_mod.reshape(depth, 1, n))


def _ffn_kernel(x_ref, sh_ref, sc_ref, gt_ref, ga_ref, gb_ref, w1_ref, w3_ref, w2_ref,
                o_ref, h_ref, acc_ref):
    j = pl.program_id(1)

    @pl.when(j == 0)
    def _():
        h = _rms(x_ref[...], ga_ref[...]) * (1.0 + sc_ref[...]) + sh_ref[...]
        h_ref[...] = h.astype(BF16)
        acc_ref[...] = jnp.zeros_like(acc_ref)

    h = h_ref[...]
    gate = _dot(h, w1_ref[...])
    up = _dot(h, w3_ref[...])
    act = (_silu(gate) * up).astype(BF16)
    acc_ref[...] += _dot(act, w2_ref[...])

    @pl.when(j == pl.num_programs(1) - 1)
    def _():
        o_ref[...] = x_ref[...] + (0.5 * gt_ref[...]) * _rms(acc_ref[...], gb_ref[...])


def _ffn(x, shift, scale, gate, gain_a, gain_b, w13, w2, layer, tiles_per_batch, tm, tf):
    n, d = x.shape
    dff = w2.shape[1]
    nf = dff // tf
    row = lambda i, j: (i, 0)
    per_batch = lambda i, j: (i // tiles_per_batch, 0, 0)
    const = lambda i, j: (0, 0)
    return pl.pallas_call(
        _ffn_kernel,
        grid=(n // tm, nf),
        in_specs=[
            pl.BlockSpec((tm, d), row),
            pl.BlockSpec((None, 1, d), per_batch),
            pl.BlockSpec((None, 1, d), per_batch),
            pl.BlockSpec((None, 1, d), per_batch),
            pl.BlockSpec((1, d), const),
            pl.BlockSpec((1, d), const),
            pl.BlockSpec((None, d, tf), lambda i, j: (layer, 0, j)),
            pl.BlockSpec((None, d, tf), lambda i, j: (layer, 0, nf + j)),
            pl.BlockSpec((None, tf, d), lambda i, j: (layer, j, 0)),
        ],
        out_specs=pl.BlockSpec((tm, d), row),
        out_shape=jax.ShapeDtypeStruct((n, d), F32),
        scratch_shapes=[pltpu.VMEM((tm, d), BF16), pltpu.VMEM((tm, d), F32)],
        compiler_params=pltpu.CompilerParams(
            dimension_semantics=("parallel", "arbitrary"), vmem_limit_bytes=VMEM_LIMIT),
        name="ffn",
    )(x, shift, scale, gate, gain_a, gain_b, w13, w13, w2)


def _inproj_kernel(x_ref, sh_ref, sc_ref, ga_ref, w_ref, o_ref, h_ref):
    @pl.when(pl.program_id(1) == 0)
    def _():
        h = _rms(x_ref[...], ga_ref[...]) * (1.0 + sc_ref[...]) + sh_ref[...]
        h_ref[...] = h.astype(BF16)

    o_ref[...] = _dot(h_ref[...], w_ref[...])


def _inproj(x, shift, scale, gain, w_in, layer, tiles_per_batch, tm, tn):
    n, d = x.shape
    cols = w_in.shape[2]
    per_batch = lambda i, j: (i // tiles_per_batch, 0, 0)
    return pl.pallas_call(
        _inproj_kernel,
        grid=(n // tm, cols // tn),
        in_specs=[
            pl.BlockSpec((tm, d), lambda i, j: (i, 0)),
            pl.BlockSpec((None, 1, d), per_batch),
            pl.BlockSpec((None, 1, d), per_batch),
            pl.BlockSpec((1, d), lambda i, j: (0, 0)),
            pl.BlockSpec((None, d, tn), lambda i, j: (layer, 0, j)),
        ],
        out_specs=pl.BlockSpec((tm, tn), lambda i, j: (i, j)),
        out_shape=jax.ShapeDtypeStruct((n, cols), F32),
        scratch_shapes=[pltpu.VMEM((tm, d), BF16)],
        compiler_params=pltpu.CompilerParams(
            dimension_semantics=("parallel", "arbitrary"), vmem_limit_bytes=VMEM_LIMIT),
        name="inproj",
    )(x, shift, scale, gain, w_in)


def _dilated_log_counts(tq, tk):
    reach = max(w for w, _ in DILATED_BRANCHES)
    nd = (reach + tq - 1) // tk + 1
    d = np.arange(nd)[:, None, None]
    i = np.arange(tq)[None, :, None]
    j = np.arange(tk)[None, None, :]
    delta = d * tk + i - j
    cnt = np.zeros(delta.shape, np.float64)
    for window, dil in DILATED_BRANCHES:
        cnt += (delta >= 0) & (delta <= window) & (delta % dil == 0)
    table = np.full((nd + 1, tq, tk), NEG_BIG, np.float32)
    table[:nd] = np.where(cnt > 0, np.log(np.maximum(cnt, 1.0)), NEG_BIG)
    return table


def _attn_a_kernel(q_ref, k_ref, v_ref, lc_ref, gain_ref, o_ref, kb_ref, vb_ref, s_ref, *, tq, tk, nd):
    qi = pl.program_id(2)

    @pl.when(qi == 0)
    def _():
        kb_ref[...] = k_ref[...].astype(BF16)
        vb_ref[:, 0:HEAD_DIM] = v_ref[...].astype(BF16)
        vb_ref[:, HEAD_DIM:] = jnp.ones((vb_ref.shape[0], HEAD_DIM), BF16)

    q = q_ref[...].astype(BF16)
    scale = HEAD_DIM ** -0.5
    half = tk // 2

    m_run = jnp.full((tq, half), NEG_BIG, F32)
    offsets = []
    for d in range(nd):
        c = qi - d
        off = pl.multiple_of(jnp.maximum(c, 0) * tk, tk)
        offsets.append(off)
        t = _dot_nt(q, kb_ref[pl.ds(off, tk), :]) * scale + lc_ref[jnp.where(c >= 0, d, nd)]
        s_ref[d] = t
        m_run = jnp.maximum(m_run, jnp.maximum(t[:, :half], t[:, half:]))
    m = jnp.max(m_run, axis=-1, keepdims=True)

    acc = jnp.zeros((tq, 2 * HEAD_DIM), F32)
    for d in range(nd):
        p = jnp.exp(s_ref[d] - m)
        acc = acc + _dot(p.astype(BF16), vb_ref[pl.ds(offsets[d], tk), :])
    o = acc[:, :HEAD_DIM] / acc[:, HEAD_DIM:]
    o_ref[...] = _rms(o, gain_ref[...]).astype(BF16)


def _attn_a(proj, gains, layer, batch, seq, gh, col0, tq, tk):
    table = jnp.asarray(_dilated_log_counts(tq, tk))
    nd = table.shape[0] - 1
    nq = seq // tq
    return pl.pallas_call(
        functools.partial(_attn_a_kernel, tq=tq, tk=tk, nd=nd),
        grid=(batch, gh, nq),
        in_specs=[
            pl.BlockSpec((tq, HEAD_DIM), lambda b, h, i: (b * nq + i, col0 + h)),
            pl.BlockSpec((seq, HEAD_DIM), lambda b, h, i: (b, col0 + gh + h)),
            pl.BlockSpec((seq, HEAD_DIM), lambda b, h, i: (b, col0 + 2 * gh + h)),
            pl.BlockSpec((nd + 1, tq, tk), lambda b, h, i: (0, 0, 0)),
            pl.BlockSpec((None, None, 1, HEAD_DIM), lambda b, h, i: (layer, h, 0, 0)),
        ],
        out_specs=pl.BlockSpec((tq, HEAD_DIM), lambda b, h, i: (b * nq + i, h)),
        out_shape=jax.ShapeDtypeStruct((batch * seq, gh * HEAD_DIM), BF16),
        scratch_shapes=[pltpu.VMEM((seq, HEAD_DIM), BF16), pltpu.VMEM((seq, 2 * HEAD_DIM), BF16),
                        pltpu.VMEM((nd, tq, tk), F32)],
        compiler_params=pltpu.CompilerParams(
            dimension_semantics=("parallel", "parallel", "arbitrary"),
            vmem_limit_bytes=VMEM_LIMIT),
        name="dilated_attention",
    )(proj, proj, proj, table, gains)


STICK_UNDERFLOW = 100.0
STICK_BOUND_SLACK = 1.01


def _attn_b_kernel(q_ref, k_ref, v_ref, gain_ref, o_ref, kb_ref, vb_ref, kmax_ref, *, tq):
    qi = pl.program_id(2)
    tk = tq

    @pl.when(qi == 0)
    def _():
        k = k_ref[...]
        kb_ref[...] = k.astype(BF16)
        vb_ref[...] = v_ref[...].astype(BF16)
        k_sq = jnp.max(jnp.sum(k * k, axis=-1, keepdims=True), axis=0, keepdims=True)
        kmax_ref[...] = jnp.broadcast_to(jnp.sqrt(k_sq), kmax_ref.shape)

    qf = q_ref[...]
    q = qf.astype(BF16)
    scale = HEAD_DIM ** -0.5
    q_norm = jnp.sqrt(jnp.sum(qf * qf, axis=-1, keepdims=True))
    z_bound = (scale * STICK_BOUND_SLACK) * q_norm * kmax_ref[0:1, 0:1]
    jj = lax.broadcasted_iota(jnp.int32, (tk, tk), 0)
    ss = lax.broadcasted_iota(jnp.int32, (tk, tk), 1)
    suffix = jnp.where(jj >= ss, 1.0, 0.0).astype(BF16)

    def step(c, tail, acc, masked):
        off = pl.multiple_of(c * tk, tk)
        k = kb_ref[pl.ds(off, tk), :]
        v = vb_ref[pl.ds(off, tk), :]
        z = _dot_nt(q, k) * scale
        log_keep = _log_sigmoid(-z)
        if masked:
            causal = ss < jj
            log_keep = jnp.where(causal, log_keep, 0.0)
        lk_hi = log_keep.astype(BF16)
        lk_lo = (log_keep - lk_hi.astype(F32)).astype(BF16)
        within = _dot(lk_hi, suffix) + _dot(lk_lo, suffix)
        a = jnp.exp(z + (within + tail))
        if masked:
            a = jnp.where(causal, a, 0.0)
        acc = acc + _dot(a.astype(BF16), v)
        tail = tail + within[:, 0:1]
        return tail, acc

    def live(tail):
        return (jnp.max(tail + z_bound) > -STICK_UNDERFLOW).astype(jnp.int32)

    tail, acc = step(qi, jnp.zeros((tq, 1), F32), jnp.zeros((tq, HEAD_DIM), F32), True)

    def cond(carry):
        c, go, _, _ = carry
        return jnp.logical_and(c >= 0, go > 0)

    def body(carry):
        c, _, tail, acc = carry
        tail, acc = step(c, tail, acc, False)
        return c - 1, live(tail), tail, acc

    _, _, _, acc = lax.while_loop(cond, body, (qi - 1, live(tail), tail, acc))
    o_ref[...] = _rms(acc, gain_ref[...]).astype(BF16)


def _attn_b(proj, gains, layer, batch, seq, gh, col0, tq):
    nq = seq // tq
    return pl.pallas_call(
        functools.partial(_attn_b_kernel, tq=tq),
        grid=(batch, gh, nq),
        in_specs=[
            pl.BlockSpec((tq, HEAD_DIM), lambda b, h, i: (b * nq + i, col0 + h)),
            pl.BlockSpec((seq, HEAD_DIM), lambda b, h, i: (b, col0 + gh + h)),
            pl.BlockSpec((seq, HEAD_DIM), lambda b, h, i: (b, col0 + 2 * gh + h)),
            pl.BlockSpec((None, None, 1, HEAD_DIM), lambda b, h, i: (layer, gh + h, 0, 0)),
        ],
        out_specs=pl.BlockSpec((tq, HEAD_DIM), lambda b, h, i: (b * nq + i, h)),
        out_shape=jax.ShapeDtypeStruct((batch * seq, gh * HEAD_DIM), BF16),
        scratch_shapes=[pltpu.VMEM((seq, HEAD_DIM), BF16), pltpu.VMEM((seq, HEAD_DIM), BF16),
                        pltpu.VMEM((8, HEAD_DIM), F32)],
        compiler_params=pltpu.CompilerParams(
            dimension_semantics=("parallel", "parallel", "arbitrary"),
            vmem_limit_bytes=VMEM_LIMIT),
        name="stick_breaking_attention",
    )(proj, proj, proj, gains)


def _hgrn_chunk(q, k, v, b, state_t):
    c = HGRN_CHUNK
    nsub = c // HGRN_SUB
    o_inter = _dot_nt((q * jnp.exp(b)).astype(BF16), state_t.astype(BF16))
    row_id = lax.broadcasted_iota(jnp.int32, (HGRN_SUB, 1), 0)
    blocks = [o_inter[HGRN_SUB * i:HGRN_SUB * (i + 1)] for i in range(nsub)]
    for sub in range(nsub):
        lo = sub * HGRN_SUB
        hi = lo + HGRN_SUB
        q_d, b_d = q[lo:hi], b[lo:hi]
        q_r, b_r = q[hi:], b[hi:]
        diag = None
        rest = None
        for si in range(HGRN_SUB):
            s = lo + si
            b_s, k_s, v_s = b[s:s + 1], k[s:s + 1], v[s:s + 1]
            decay = jnp.exp(jnp.minimum(b_d - b_s, 0.0))
            score = jnp.sum(q_d * decay * k_s, axis=-1, keepdims=True)
            if si > 0:
                score = jnp.where(row_id >= si, score, 0.0)
            term = score * v_s
            diag = term if diag is None else diag + term
            if sub < nsub - 1:
                score = jnp.sum(q_r * jnp.exp(b_r - b_s) * k_s, axis=-1, keepdims=True)
                term = score * v_s
                rest = term if rest is None else rest + term
        blocks[sub] = blocks[sub] + diag
        for i in range(sub + 1, nsub):
            r = (i - sub - 1) * HGRN_SUB
            blocks[i] = blocks[i] + rest[r:r + HGRN_SUB]
    o = jnp.concatenate(blocks, axis=0)
    b_last = b[c - 1:c]
    k_decayed = k * jnp.exp(b_last - b)
    new_state_t = jnp.exp(b_last) * state_t + _dot_tn(v.astype(BF16), k_decayed.astype(BF16))
    return o, new_state_t


def _hgrn_kernel(cq_ref, cf_ref, ci_ref, cg_ref, lb_ref, gain_ref, o_ref, state_ref, of_ref, *, gh):
    ts = cq_ref.shape[0]
    c = HGRN_CHUNK

    @pl.when(pl.program_id(1) == 0)
    def _():
        state_ref[...] = jnp.zeros_like(state_ref)

    lb = lb_ref[...]
    log_lb = jnp.log(jnp.maximum(lb, LB_FLOOR))
    log_1m_lb = jnp.log1p(-lb)
    ii = lax.broadcasted_iota(jnp.int32, (c, c), 0)
    jj = lax.broadcasted_iota(jnp.int32, (c, c), 1)
    prefix = jnp.where(jj <= ii, 1.0, 0.0).astype(BF16)

    def chunk(ci, _):
        base = pl.multiple_of(ci * c, c)
        rows = pl.ds(base, c)
        cf = cf_ref[rows, :]
        x = log_1m_lb + _log_sigmoid(cf)
        hi = jnp.maximum(log_lb, x)
        log_f = hi + jnp.log1p(jnp.exp(-jnp.abs(log_lb - x)))
        b_all = _dot_exact_lhs(prefix, log_f)
        k_all = (1.0 - lb) * _sigmoid(-cf)
        q_all = _silu(cq_ref[rows, :])
        v_all = ci_ref[rows, :]
        for h in range(gh):
            cols = slice(h * HEAD_DIM, (h + 1) * HEAD_DIM)
            o, st = _hgrn_chunk(q_all[:, cols], k_all[:, cols], v_all[:, cols], b_all[:, cols],
                                state_ref[h])
            state_ref[h] = st
            of_ref[rows, cols] = o
        return 0

    lax.fori_loop(0, ts // c, chunk, 0)

    for h in range(gh):
        cols = slice(h * HEAD_DIM, (h + 1) * HEAD_DIM)
        o = _rms(of_ref[:, cols], gain_ref[:, cols]) * _silu(cg_ref[:, cols])
        o_ref[:, cols] = o.astype(BF16)


def _hgrn(proj, lower_bound, gains, layer, batch, seq, gh, col0, ts):
    gw = gh * HEAD_DIM
    nt = seq // ts
    blk = lambda k: pl.BlockSpec((ts, gw), lambda b, i, k=k: (b * nt + i, col0 + k))
    return pl.pallas_call(
        functools.partial(_hgrn_kernel, gh=gh),
        grid=(batch, nt),
        in_specs=[
            blk(0), blk(1), blk(2), blk(3),
            pl.BlockSpec((None, 1, gw), lambda b, i: (layer, 0, 0)),
            pl.BlockSpec((None, None, 1, gw), lambda b, i: (layer, 2, 0, 0)),
        ],
        out_specs=pl.BlockSpec((ts, gw), lambda b, i: (b * nt + i, 0)),
        out_shape=jax.ShapeDtypeStruct((batch * seq, gw), BF16),
        scratch_shapes=[pltpu.VMEM((gh, HEAD_DIM, HEAD_DIM), F32), pltpu.VMEM((ts, gw), F32)],
        compiler_params=pltpu.CompilerParams(
            dimension_semantics=("parallel", "arbitrary"), vmem_limit_bytes=VMEM_LIMIT),
        name="hgrn2",
    )(proj, proj, proj, proj, lower_bound, gains)


def _gdn_kernel(xq_ref, xk_ref, xv_ref, z_ref, ab_ref, wq_ref, wk_ref, wv_ref, alog_ref, dt_ref,
                gain_ref, o_ref, pad_ref, act_ref, state_ref, of_ref, *, gh):
    ts = xq_ref.shape[0]
    c = GDN_CHUNK
    first = pl.program_id(1) == 0

    @pl.when(first)
    def _():
        state_ref[...] = jnp.zeros_like(state_ref)
        pad_ref[:, 0:CONV_HALO, :] = jnp.zeros((3, CONV_HALO, pad_ref.shape[2]), F32)

    @pl.when(jnp.logical_not(first))
    def _():
        pad_ref[:, 0:CONV_HALO, :] = pad_ref[:, ts:ts + CONV_HALO, :]

    for part, (x_ref, w_ref) in enumerate(((xq_ref, wq_ref), (xk_ref, wk_ref), (xv_ref, wv_ref))):
        pad_ref[part, CONV_HALO:CONV_HALO + ts, :] = x_ref[...]
        acc = None
        for tap in range(CONV_WIDTH):
            start = CONV_HALO - (CONV_WIDTH - 1) + tap
            term = w_ref[tap:tap + 1, :] * pad_ref[part, start:start + ts, :]
            acc = term if acc is None else acc + term
        act_ref[part] = _silu(acc)

    ab = ab_ref[:, 0:HEAD_DIM]
    log_alpha = -jnp.exp(alog_ref[...]) * _softplus(ab + dt_ref[...])
    beta_all = _sigmoid(ab)

    ii = lax.broadcasted_iota(jnp.int32, (c, c), 0)
    jj = lax.broadcasted_iota(jnp.int32, (c, c), 1)
    incl = jj <= ii
    strict = jj < ii
    prefix = jnp.where(incl, 1.0, 0.0).astype(BF16)
    base_shift = int(np.log2(GDN_INV_BASE))
    in_base = jnp.where((ii >> base_shift) == (jj >> base_shift), 1.0, 0.0)
    merge_masks = []
    for shift in range(base_shift, int(np.log2(c))):
        same_pair = (ii >> (shift + 1)) == (jj >> (shift + 1))
        other_half = (ii >> shift) != (jj >> shift)
        merge_masks.append(jnp.where(same_pair & other_half, 1.0, 0.0))

    n_chunks = ts // c
    bodies = [(ci, h) for ci in range(n_chunks) for h in range(gh)]
    g_all, g_rows = [], []
    for ci in range(n_chunks):
        g = _dot_exact_lhs(prefix, log_alpha[ci * c:(ci + 1) * c])
        g_all.append(g)
        g_rows.append(g.T)

    def load(part, ci, h):
        return act_ref[part, ci * c:(ci + 1) * c, h * HEAD_DIM:(h + 1) * HEAD_DIM]

    def l2norm(t):
        return t * lax.rsqrt(jnp.sum(t * t, axis=-1, keepdims=True) + EPS)

    q = [l2norm(load(0, ci, h)) * (HEAD_DIM ** -0.5) for ci, h in bodies]
    k = [l2norm(load(1, ci, h)) for ci, h in bodies]
    v = [load(2, ci, h) for ci, h in bodies]
    beta = [beta_all[ci * c:(ci + 1) * c, gh + h:gh + h + 1] for ci, h in bodies]
    g_col = [g_all[ci][:, h:h + 1] for ci, h in bodies]
    g_last = [g_all[ci][c - 1:c, h:h + 1] for ci, h in bodies]
    decay = [jnp.where(incl, jnp.exp(jnp.minimum(gc - g_rows[ci][h:h + 1, :], 0.0)), 0.0)
             for gc, (ci, h) in zip(g_col, bodies)]
    k_beta = [ki * bi for ki, bi in zip(k, beta)]
    k16 = [ki.astype(BF16) for ki in k]
    a = [jnp.where(strict, _dot_nt(kb.astype(BF16), kh) * dc, 0.0)
         for kb, kh, dc in zip(k_beta, k16, decay)]
    power = [ai * in_base for ai in a]
    n = [-p for p in power]
    for _ in range(base_shift - 1):
        power = [_dot16(p, p) for p in power]
        n = [ni + p + _dot16(ni, p) for ni, p in zip(n, power)]
    for mask in merge_masks:
        off = [ai * mask for ai in a]
        x = [o + _dot16(ni, o) for ni, o in zip(n, off)]
        n = [ni - (xi + _dot16(xi, ni)) for ni, xi in zip(n, x)]
    rhs = [jnp.concatenate([vi * bi, kb * jnp.exp(gc)], axis=1)
           for vi, bi, kb, gc in zip(v, beta, k_beta, g_col)]
    uw = [r + _dot16(ni, r) for ni, r in zip(n, rhs)]
    qk = [(_dot_nt(qi.astype(BF16), kh) * dc).astype(BF16) for qi, kh, dc in zip(q, k16, decay)]
    q_in = [(qi * jnp.exp(gc)).astype(BF16) for qi, gc in zip(q, g_col)]
    k_out = [(ki * jnp.exp(gl - gc)).astype(BF16) for ki, gl, gc in zip(k, g_last, g_col)]

    for i, (ci, h) in enumerate(bodies):
        state = state_ref[h]
        s16 = state.astype(BF16)
        v_new = uw[i][:, :HEAD_DIM] - _dot(uw[i][:, HEAD_DIM:].astype(BF16), s16)
        vn16 = v_new.astype(BF16)
        o = _dot(q_in[i], s16) + _dot(qk[i], vn16)
        state_ref[h] = jnp.exp(g_last[i]) * state + _dot_tn(k_out[i], vn16)
        of_ref[ci * c:(ci + 1) * c, h * HEAD_DIM:(h + 1) * HEAD_DIM] = o

    for h in range(gh):
        cols = slice(h * HEAD_DIM, (h + 1) * HEAD_DIM)
        o = _rms(of_ref[:, cols], gain_ref[:, cols]) * _silu(z_ref[:, cols])
        o_ref[:, cols] = o.astype(BF16)


def _gdn(proj, conv_w, alog_row, dt_row, gains, layer, batch, seq, gh, col0, ts):
    gw = gh * HEAD_DIM
    nt = seq // ts
    blk = lambda k: pl.BlockSpec((ts, gw), lambda b, i, k=k: (b * nt + i, col0 + k))
    wblk = lambda k: pl.BlockSpec((None, CONV_WIDTH, gw), lambda b, i, k=k: (layer, 0, k))
    row = pl.BlockSpec((None, 1, HEAD_DIM), lambda b, i: (layer, 0, 0))
    return pl.pallas_call(
        functools.partial(_gdn_kernel, gh=gh),
        grid=(batch, nt),
        in_specs=[
            blk(0), blk(1), blk(2), blk(3), blk(4),
            wblk(0), wblk(1), wblk(2), row, row,
            pl.BlockSpec((None, None, 1, gw), lambda b, i: (layer, 3, 0, 0)),
        ],
        out_specs=pl.BlockSpec((ts, gw), lambda b, i: (b * nt + i, 0)),
        out_shape=jax.ShapeDtypeStruct((batch * seq, gw), BF16),
        scratch_shapes=[
            pltpu.VMEM((3, ts + CONV_HALO, gw), F32),
            pltpu.VMEM((3, ts, gw), F32),
            pltpu.VMEM((gh, HEAD_DIM, HEAD_DIM), F32),
            pltpu.VMEM((ts, gw), F32),
        ],
        compiler_params=pltpu.CompilerParams(
            dimension_semantics=("parallel", "arbitrary"), vmem_limit_bytes=VMEM_LIMIT),
        name="gated_deltanet",
    )(proj, proj, proj, proj, proj, conv_w, conv_w, conv_w, alog_row, dt_row, gains)


def _outproj_kernel(x_ref, oa_ref, ob_ref, oc_ref, od_ref, w_ref, gt_ref, gain_ref, o_ref):
    gw = oa_ref.shape[1]
    y = None
    for g, m_ref in enumerate((oa_ref, ob_ref, oc_ref, od_ref)):
        part = _dot(m_ref[...], w_ref[g * gw:(g + 1) * gw, :])
        y = part if y is None else y + part
    o_ref[...] = x_ref[...] + gt_ref[...] * _rms(y, gain_ref[...])


def _outproj(x, groups, w_out, gate, gain, layer, tiles_per_batch, tm):
    n, d = x.shape
    gw = groups[0].shape[1]
    gspec = pl.BlockSpec((tm, gw), lambda i: (i, 0))
    return pl.pallas_call(
        _outproj_kernel,
        grid=(n // tm,),
        in_specs=[
            pl.BlockSpec((tm, d), lambda i: (i, 0)),
            gspec, gspec, gspec, gspec,
            pl.BlockSpec((None, N_GROUPS * gw, d), lambda i: (layer, 0, 0)),
            pl.BlockSpec((None, 1, d), lambda i: (i // tiles_per_batch, 0, 0)),
            pl.BlockSpec((1, d), lambda i: (0, 0)),
        ],
        out_specs=pl.BlockSpec((tm, d), lambda i: (i, 0)),
        out_shape=jax.ShapeDtypeStruct((n, d), F32),
        compiler_params=pltpu.CompilerParams(
            dimension_semantics=("parallel",), vmem_limit_bytes=VMEM_LIMIT),
        name="outproj",
    )(x, *groups, w_out, gate, gain)


def _largest_tile(n, cap):
    t = min(n, cap)
    while n % t:
        t //= 2
    return t


def kernel(x, c, w_mod, b_mod, norm_gain, w_in, w_out, mix_out_gain, hgrn_lb_logits, dn_conv_w,
           dn_a_log, dn_dt_bias, ffn1_w13, ffn1_w2, ffn2_w13, ffn2_w2):
    batch, seq, d = x.shape
    depth = w_mod.shape[0]
    gw = d // N_GROUPS
    gh = gw // HEAD_DIM
    dff = ffn1_w2.shape[1]
    n = batch * seq

    tm_ffn = _largest_tile(seq, 512)
    tf = dff // 11 if dff % 11 == 0 and (dff // 11) % 128 == 0 else _largest_tile(dff, 512)
    tm_proj = _largest_tile(seq, 1024)
    tm_out = _largest_tile(seq, 512)
    tq = _largest_tile(seq, 256)
    ts_c = _largest_tile(seq, 1024)
    ts_d = _largest_tile(seq, 512)

    in_cols = w_in.shape[2]
    pad_cols = (-in_cols) % gw
    w_in16 = jnp.pad(w_in, ((0, 0), (0, 0), (0, pad_cols))).astype(BF16)
    w_out16 = w_out.astype(BF16)
    ffn_w = [(ffn1_w13.astype(BF16), ffn1_w2.astype(BF16)),
             (ffn2_w13.astype(BF16), ffn2_w2.astype(BF16))]
    lb_p = jax.nn.softmax(hgrn_lb_logits.astype(F32), axis=0)
    lower_bounds = (jnp.cumsum(lb_p, axis=0) - lb_p[0]).reshape(depth, 1, gw)
    gains = mix_out_gain.reshape(depth, N_GROUPS, 1, gw)
    head_gains = mix_out_gain.reshape(depth, N_GROUPS * gh, 1, HEAD_DIM)
    lane_pad = ((0, 0), (0, HEAD_DIM - gh))
    alog_row = jnp.pad(dn_a_log.astype(F32), lane_pad).reshape(depth, 1, HEAD_DIM)
    dt_row = jnp.pad(dn_dt_bias.astype(F32), lane_pad).reshape(depth, 1, HEAD_DIM)
    conv_w = dn_conv_w.astype(F32)

    c_rows = -(-batch // 8) * 8
    c_pad = jnp.pad(c, ((0, c_rows - batch), (0, 0)))
    mod = _modulation(c_pad, w_mod, b_mod, _largest_tile(N_MOD * d, 1024))[:, :batch]
    mod = mod.reshape(depth, batch, N_MOD, 1, d)

    xf = x.reshape(n, d)
    for l in range(depth):
        sh1, sc1, g1, sh2, sc2, g2, sh3, sc3, g3 = (mod[l, :, i] for i in range(N_MOD))
        ng = norm_gain[l].reshape(6, 1, d)
        xf = _ffn(xf, sh1, sc1, g1, ng[0], ng[1], *ffn_w[0], l, seq // tm_ffn, tm_ffn, tf)
        proj = _inproj(xf, sh2, sc2, ng[2], w_in16, l, seq // tm_proj, tm_proj, gw)
        hb = gw // HEAD_DIM
        o_a = _attn_a(proj, head_gains, l, batch, seq, gh, 0 * hb, tq, tq)
        o_b = _attn_b(proj, head_gains, l, batch, seq, gh, 3 * hb, tq)
        o_c = _hgrn(proj, lower_bounds, gains, l, batch, seq, gh, 6, ts_c)
        o_d = _gdn(proj, conv_w, alog_row, dt_row, gains, l, batch, seq, gh, 10, ts_d)
        xf = _outproj(xf, (o_a, o_b, o_c, o_d), w_out16, g2, ng[3], l, seq // tm_out, tm_out)
        xf = _ffn(xf, sh3, sc3, g3, ng[4], ng[5], *ffn_w[1], l, seq // tm_ffn, tm_ffn, tf)
    return xf.reshape(batch, seq, d)
```

```python
import functools

import jax
import jax.numpy as jnp
import numpy as np
from jax import lax
from jax.experimental import pallas as pl
from jax.experimental.pallas import tpu as pltpu

F32 = jnp.float32
BF16 = jnp.bfloat16

HEAD_DIM = 128
N_GROUPS = 4
N_MOD = 9
CONV_WIDTH = 4
DILATED_BRANCHES = ((128, 1), (512, 4), (2048, 16))
EPS = 1e-6
NEG_BIG = -1e30
LB_FLOOR = 1e-30

HGRN_CHUNK = 32
HGRN_SUB = 8
GDN_CHUNK = 128
GDN_INV_BASE = 8
CONV_HALO = 8
NORM_ROWS = 128
VMEM_LIMIT = 56 * 1024 * 1024

_NT = (((1,), (1,)), ((), ()))
_TN = (((0,), (0,)), ((), ()))


def _dot(a, b):
    return jnp.dot(a, b, preferred_element_type=F32)


def _dot_nt(a, b):
    return lax.dot_general(a, b, _NT, preferred_element_type=F32)


def _dot_tn(a, b):
    return lax.dot_general(a, b, _TN, preferred_element_type=F32)


def _split3(a):
    hi = a.astype(BF16)
    r = a - hi.astype(F32)
    mid = r.astype(BF16)
    lo = (r - mid.astype(F32)).astype(BF16)
    return hi, mid, lo


def _dot_exact_lhs(a_bf16, b):
    hi, mid, lo = _split3(b)
    return _dot(a_bf16, hi) + (_dot(a_bf16, mid) + _dot(a_bf16, lo))


def _dot16(a, b):
    return _dot(a.astype(BF16), b.astype(BF16))


def _sigmoid(x):
    return 1.0 / (1.0 + jnp.exp(-x))


def _silu(x):
    return x * _sigmoid(x)


def _softplus(x):
    return jnp.maximum(x, 0.0) + jnp.log1p(jnp.exp(-jnp.abs(x)))


def _log_sigmoid(x):
    return jnp.minimum(x, 0.0) - jnp.log1p(jnp.exp(-jnp.abs(x)))


def _rms(x, gain):
    return x * lax.rsqrt(jnp.mean(x * x, axis=-1, keepdims=True) + EPS) * gain


def _mod_kernel(c_ref, w_ref, b_ref, o_ref):
    cond = _silu(c_ref[...]).astype(BF16)
    o_ref[...] = _dot(cond, w_ref[...].astype(BF16)) + b_ref[...]


def _modulation(c_pad, w_mod, b_mod, tn):
    depth, d, n = w_mod.shape
    rows = c_pad.shape[0]
    return pl.pallas_call(
        _mod_kernel,
        grid=(depth, n // tn),
        in_specs=[
            pl.BlockSpec((rows, d), lambda l, j: (0, 0)),
            pl.BlockSpec((None, d, tn), lambda l, j: (l, 0, j)),
            pl.BlockSpec((None, 1, tn), lambda l, j: (l, 0, j)),
        ],
        out_specs=pl.BlockSpec((None, rows, tn), lambda l, j: (l, 0, j)),
        out_shape=jax.ShapeDtypeStruct((depth, rows, n), F32),
        compiler_params=pltpu.CompilerParams(
            dimension_semantics=("parallel", "parallel"), vmem_limit_bytes=VMEM_LIMIT),
        name="modulation",
    )(c_pad, w_mod, b_mod.reshape(depth, 1, n))


def _norm_modulate(x, gain, shift, scale):
    return (_rms(x, gain) * (1.0 + scale) + shift).astype(BF16)


def _for_row_slices(n_rows, body):
    def step(r, carry):
        body(pl.ds(pl.multiple_of(r * NORM_ROWS, NORM_ROWS), NORM_ROWS))
        return carry

    lax.fori_loop(0, n_rows // NORM_ROWS, step, 0)


def _inv_rms(x):
    return lax.rsqrt(jnp.mean(x * x, axis=-1, keepdims=True) + EPS)


def _norm_modulate_tile(x_ref, r_ref, ga_ref, sh_ref, sc_ref, h_ref, also=None):
    r_ref[...] = _inv_rms(x_ref[...])

    def body(rows):
        y = x_ref[rows, :] * r_ref[rows, :] * ga_ref[...]
        h_ref[rows, :] = (y * (1.0 + sc_ref[...]) + sh_ref[...]).astype(BF16)
        if also is not None:
            also(rows)

    _for_row_slices(x_ref.shape[0], body)


def _ffn_kernel(x_ref, sh_ref, sc_ref, gt_ref, ga_ref, gb_ref, w1_ref, w3_ref, w2_ref,
                o_ref, h_ref, acc_ref, r_ref):
    j = pl.program_id(1)

    @pl.when(j == 0)
    def _():
        def clear(rows):
            acc_ref[rows, :] = jnp.zeros((NORM_ROWS, acc_ref.shape[1]), F32)

        _norm_modulate_tile(x_ref, r_ref, ga_ref, sh_ref, sc_ref, h_ref, also=clear)

    h = h_ref[...]
    gate = _dot(h, w1_ref[...])
    up = _dot(h, w3_ref[...])
    act = (_silu(gate) * up).astype(BF16)
    acc_ref[...] += _dot(act, w2_ref[...])

    @pl.when(j == pl.num_programs(1) - 1)
    def _():
        r_ref[...] = _inv_rms(acc_ref[...])

        def body(rows):
            y = acc_ref[rows, :] * r_ref[rows, :] * gb_ref[...]
            o_ref[rows, :] = x_ref[rows, :] + (0.5 * gt_ref[...]) * y

        _for_row_slices(x_ref.shape[0], body)


def _ffn(x, shift, scale, gate, gain_a, gain_b, w13, w2, layer, tiles_per_batch, tm, tf):
    n, d = x.shape
    dff = w2.shape[1]
    nf = dff // tf
    row = lambda i, j: (i, 0)
    per_batch = lambda i, j: (i // tiles_per_batch, 0, 0)
    const = lambda i, j: (0, 0)
    return pl.pallas_call(
        _ffn_kernel,
        grid=(n // tm, nf),
        in_specs=[
            pl.BlockSpec((tm, d), row),
            pl.BlockSpec((None, 1, d), per_batch),
            pl.BlockSpec((None, 1, d), per_batch),
            pl.BlockSpec((None, 1, d), per_batch),
            pl.BlockSpec((1, d), const),
            pl.BlockSpec((1, d), const),
            pl.BlockSpec((None, d, tf), lambda i, j: (layer, 0, j)),
            pl.BlockSpec((None, d, tf), lambda i, j: (layer, 0, nf + j)),
            pl.BlockSpec((None, tf, d), lambda i, j: (layer, j, 0)),
        ],
        out_specs=pl.BlockSpec((tm, d), row),
        out_shape=jax.ShapeDtypeStruct((n, d), F32),
        scratch_shapes=[pltpu.VMEM((tm, d), BF16), pltpu.VMEM((tm, d), F32),
                        pltpu.VMEM((tm, 1), F32)],
        compiler_params=pltpu.CompilerParams(
            dimension_semantics=("parallel", "arbitrary"), vmem_limit_bytes=VMEM_LIMIT),
        name="ffn",
    )(x, shift, scale, gate, gain_a, gain_b, w13, w13, w2)


def _inproj_kernel(x_ref, sh_ref, sc_ref, ga_ref, w_ref, o_ref, h_ref, r_ref):
    @pl.when(pl.program_id(1) == 0)
    def _():
        _norm_modulate_tile(x_ref, r_ref, ga_ref, sh_ref, sc_ref, h_ref)

    o_ref[...] = _dot(h_ref[...], w_ref[...])


def _inproj(x, shift, scale, gain, w_in, layer, tiles_per_batch, tm, tn):
    n, d = x.shape
    cols = w_in.shape[2]
    per_batch = lambda i, j: (i // tiles_per_batch, 0, 0)
    return pl.pallas_call(
        _inproj_kernel,
        grid=(n // tm, cols // tn),
        in_specs=[
            pl.BlockSpec((tm, d), lambda i, j: (i, 0)),
            pl.BlockSpec((None, 1, d), per_batch),
            pl.BlockSpec((None, 1, d), per_batch),
            pl.BlockSpec((1, d), lambda i, j: (0, 0)),
            pl.BlockSpec((None, d, tn), lambda i, j: (layer, 0, j)),
        ],
        out_specs=pl.BlockSpec((tm, tn), lambda i, j: (i, j)),
        out_shape=jax.ShapeDtypeStruct((n, cols), F32),
        scratch_shapes=[pltpu.VMEM((tm, d), BF16), pltpu.VMEM((tm, 1), F32)],
        compiler_params=pltpu.CompilerParams(
            dimension_semantics=("parallel", "arbitrary"), vmem_limit_bytes=VMEM_LIMIT),
        name="inproj",
    )(x, shift, scale, gain, w_in)


def _dilated_log_counts(tq, tk):
    reach = max(w for w, _ in DILATED_BRANCHES)
    nd = (reach + tq - 1) // tk + 1
    d = np.arange(nd)[:, None, None]
    i = np.arange(tq)[None, :, None]
    j = np.arange(tk)[None, None, :]
    delta = d * tk + i - j
    cnt = np.zeros(delta.shape, np.float64)
    for window, dil in DILATED_BRANCHES:
        cnt += (delta >= 0) & (delta <= window) & (delta % dil == 0)
    table = np.full((nd + 1, tq, tk), NEG_BIG, np.float32)
    table[:nd] = np.where(cnt > 0, np.log(np.maximum(cnt, 1.0)), NEG_BIG)
    return table


def _attn_a_kernel(q_ref, k_ref, v_ref, lc_ref, gain_ref, o_ref, kb_ref, vb_ref, s_ref, *, tq, tk, nd):
    qi = pl.program_id(2)

    @pl.when(qi == 0)
    def _():
        kb_ref[...] = k_ref[...].astype(BF16)
        vb_ref[:, 0:HEAD_DIM] = v_ref[...].astype(BF16)
        vb_ref[:, HEAD_DIM:] = jnp.ones((vb_ref.shape[0], HEAD_DIM), BF16)

    q = q_ref[...].astype(BF16)
    scale = HEAD_DIM ** -0.5
    half = tk // 2

    m_run = jnp.full((tq, half), NEG_BIG, F32)
    offsets = []
    for d in range(nd):
        c = qi - d
        off = pl.multiple_of(jnp.maximum(c, 0) * tk, tk)
        offsets.append(off)
        t = _dot_nt(q, kb_ref[pl.ds(off, tk), :]) * scale + lc_ref[jnp.where(c >= 0, d, nd)]
        s_ref[d] = t
        m_run = jnp.maximum(m_run, jnp.maximum(t[:, :half], t[:, half:]))
    m = jnp.max(m_run, axis=-1, keepdims=True)

    acc = jnp.zeros((tq, 2 * HEAD_DIM), F32)
    for d in range(nd):
        p = jnp.exp(s_ref[d] - m)
        acc = acc + _dot(p.astype(BF16), vb_ref[pl.ds(offsets[d], tk), :])
    o = acc[:, :HEAD_DIM] / acc[:, HEAD_DIM:]
    o_ref[...] = _rms(o, gain_ref[...]).astype(BF16)


def _attn_a(proj, gains, layer, batch, seq, gh, col0, tq, tk):
    table = jnp.asarray(_dilated_log_counts(tq, tk))
    nd = table.shape[0] - 1
    nq = seq // tq
    return pl.pallas_call(
        functools.partial(_attn_a_kernel, tq=tq, tk=tk, nd=nd),
        grid=(batch, gh, nq),
        in_specs=[
            pl.BlockSpec((tq, HEAD_DIM), lambda b, h, i: (b * nq + i, col0 + h)),
            pl.BlockSpec((seq, HEAD_DIM), lambda b, h, i: (b, col0 + gh + h)),
            pl.BlockSpec((seq, HEAD_DIM), lambda b, h, i: (b, col0 + 2 * gh + h)),
            pl.BlockSpec((nd + 1, tq, tk), lambda b, h, i: (0, 0, 0)),
            pl.BlockSpec((None, None, 1, HEAD_DIM), lambda b, h, i: (layer, h, 0, 0)),
        ],
        out_specs=pl.BlockSpec((tq, HEAD_DIM), lambda b, h, i: (b * nq + i, h)),
        out_shape=jax.ShapeDtypeStruct((batch * seq, gh * HEAD_DIM), BF16),
        scratch_shapes=[pltpu.VMEM((seq, HEAD_DIM), BF16), pltpu.VMEM((seq, 2 * HEAD_DIM), BF16),
                        pltpu.VMEM((nd, tq, tk), F32)],
        compiler_params=pltpu.CompilerParams(
            dimension_semantics=("parallel", "parallel", "arbitrary"),
            vmem_limit_bytes=VMEM_LIMIT),
        name="dilated_attention",
    )(proj, proj, proj, table, gains)


STICK_UNDERFLOW = 100.0
STICK_BOUND_SLACK = 1.01


def _attn_b_kernel(q_ref, k_ref, v_ref, gain_ref, o_ref, kb_ref, vb_ref, kmax_ref, *, tq):
    qi = pl.program_id(2)
    tk = tq

    @pl.when(qi == 0)
    def _():
        k = k_ref[...]
        kb_ref[...] = k.astype(BF16)
        vb_ref[...] = v_ref[...].astype(BF16)
        k_sq = jnp.max(jnp.sum(k * k, axis=-1, keepdims=True), axis=0, keepdims=True)
        kmax_ref[...] = jnp.broadcast_to(jnp.sqrt(k_sq), kmax_ref.shape)

    qf = q_ref[...]
    q = qf.astype(BF16)
    scale = HEAD_DIM ** -0.5
    q_norm = jnp.sqrt(jnp.sum(qf * qf, axis=-1, keepdims=True))
    z_bound = (scale * STICK_BOUND_SLACK) * q_norm * kmax_ref[0:1, 0:1]
    jj = lax.broadcasted_iota(jnp.int32, (tk, tk), 0)
    ss = lax.broadcasted_iota(jnp.int32, (tk, tk), 1)
    suffix = jnp.where(jj >= ss, 1.0, 0.0).astype(BF16)

    def logits(c, keep):
        off = pl.multiple_of(c * tk, tk)
        z = _dot_nt(q, kb_ref[pl.ds(off, tk), :]) * scale
        log_keep = _log_sigmoid(-z)
        if keep is not None:
            log_keep = jnp.where(keep, log_keep, 0.0)
        return z, log_keep, vb_ref[pl.ds(off, tk), :]

    def suffix_sums(log_keep):
        lk_hi = log_keep.astype(BF16)
        lk_lo = (log_keep - lk_hi.astype(F32)).astype(BF16)
        return _dot(lk_hi, suffix) + _dot(lk_lo, suffix)

    def live(tail):
        return (jnp.max(tail + z_bound) > -STICK_UNDERFLOW).astype(jnp.int32)

    causal = ss < jj
    has_left = qi > 0
    z_d, lk_d, v_d = logits(qi, causal)
    z_l, lk_l, v_l = logits(jnp.maximum(qi - 1, 0), has_left)
    within_d = suffix_sums(lk_d)
    within_l = suffix_sums(lk_l) + within_d[:, 0:1]
    a_d = jnp.where(causal, jnp.exp(z_d + within_d), 0.0)
    a_l = jnp.where(has_left, jnp.exp(z_l + within_l), 0.0)
    acc = _dot(a_d.astype(BF16), v_d) + _dot(a_l.astype(BF16), v_l)
    tail = within_l[:, 0:1]

    def cond(carry):
        c, go, _, _ = carry
        return jnp.logical_and(c >= 0, go > 0)

    def body(carry):
        c, _, tail, acc = carry
        z, log_keep, v = logits(c, None)
        within = suffix_sums(log_keep) + tail
        acc = acc + _dot(jnp.exp(z + within).astype(BF16), v)
        tail = within[:, 0:1]
        return c - 1, live(tail), tail, acc

    _, _, _, acc = lax.while_loop(cond, body, (qi - 2, live(tail), tail, acc))
    o_ref[...] = _rms(acc, gain_ref[...]).astype(BF16)


def _attn_b(proj, gains, layer, batch, seq, gh, col0, tq):
    nq = seq // tq
    return pl.pallas_call(
        functools.partial(_attn_b_kernel, tq=tq),
        grid=(batch, gh, nq),
        in_specs=[
            pl.BlockSpec((tq, HEAD_DIM), lambda b, h, i: (b * nq + i, col0 + h)),
            pl.BlockSpec((seq, HEAD_DIM), lambda b, h, i: (b, col0 + gh + h)),
            pl.BlockSpec((seq, HEAD_DIM), lambda b, h, i: (b, col0 + 2 * gh + h)),
            pl.BlockSpec((None, None, 1, HEAD_DIM), lambda b, h, i: (layer, gh + h, 0, 0)),
        ],
        out_specs=pl.BlockSpec((tq, HEAD_DIM), lambda b, h, i: (b * nq + i, h)),
        out_shape=jax.ShapeDtypeStruct((batch * seq, gh * HEAD_DIM), BF16),
        scratch_shapes=[pltpu.VMEM((seq, HEAD_DIM), BF16), pltpu.VMEM((seq, HEAD_DIM), BF16),
                        pltpu.VMEM((8, HEAD_DIM), F32)],
        compiler_params=pltpu.CompilerParams(
            dimension_semantics=("parallel", "parallel", "arbitrary"),
            vmem_limit_bytes=VMEM_LIMIT),
        name="stick_breaking_attention",
    )(proj, proj, proj, gains)


def _hgrn_chunk(q, k, v, b, state_t):
    c = HGRN_CHUNK
    nsub = c // HGRN_SUB
    o_inter = _dot_nt((q * jnp.exp(b)).astype(BF16), state_t.astype(BF16))
    row_id = lax.broadcasted_iota(jnp.int32, (HGRN_SUB, 1), 0)
    blocks = [o_inter[HGRN_SUB * i:HGRN_SUB * (i + 1)] for i in range(nsub)]
    for sub in range(nsub):
        lo = sub * HGRN_SUB
        hi = lo + HGRN_SUB
        q_d, b_d = q[lo:hi], b[lo:hi]
        q_r, b_r = q[hi:], b[hi:]
        diag = None
        rest = None
        for si in range(HGRN_SUB):
            s = lo + si
            b_s, k_s, v_s = b[s:s + 1], k[s:s + 1], v[s:s + 1]
            decay = jnp.exp(jnp.minimum(b_d - b_s, 0.0))
            score = jnp.sum(q_d * decay * k_s, axis=-1, keepdims=True)
            if si > 0:
                score = jnp.where(row_id >= si, score, 0.0)
            term = score * v_s
            diag = term if diag is None else diag + term
            if sub < nsub - 1:
                score = jnp.sum(q_r * jnp.exp(b_r - b_s) * k_s, axis=-1, keepdims=True)
                term = score * v_s
                rest = term if rest is None else rest + term
        blocks[sub] = blocks[sub] + diag
        for i in range(sub + 1, nsub):
            r = (i - sub - 1) * HGRN_SUB
            blocks[i] = blocks[i] + rest[r:r + HGRN_SUB]
    o = jnp.concatenate(blocks, axis=0)
    b_last = b[c - 1:c]
    k_decayed = k * jnp.exp(b_last - b)
    new_state_t = jnp.exp(b_last) * state_t + _dot_tn(v.astype(BF16), k_decayed.astype(BF16))
    return o, new_state_t


def _hgrn_kernel(cq_ref, cf_ref, ci_ref, cg_ref, lb_ref, gain_ref, o_ref, state_ref, of_ref, *, gh):
    ts = cq_ref.shape[0]
    c = HGRN_CHUNK

    @pl.when(pl.program_id(1) == 0)
    def _():
        state_ref[...] = jnp.zeros_like(state_ref)

    lb = lb_ref[...]
    log_lb = jnp.log(jnp.maximum(lb, LB_FLOOR))
    log_1m_lb = jnp.log1p(-lb)
    ii = lax.broadcasted_iota(jnp.int32, (c, c), 0)
    jj = lax.broadcasted_iota(jnp.int32, (c, c), 1)
    prefix = jnp.where(jj <= ii, 1.0, 0.0).astype(BF16)

    def chunk(ci, _):
        base = pl.multiple_of(ci * c, c)
        rows = pl.ds(base, c)
        cf = cf_ref[rows, :]
        x = log_1m_lb + _log_sigmoid(cf)
        hi = jnp.maximum(log_lb, x)
        log_f = hi + jnp.log1p(jnp.exp(-jnp.abs(log_lb - x)))
        b_all = _dot_exact_lhs(prefix, log_f)
        k_all = (1.0 - lb) * _sigmoid(-cf)
        q_all = _silu(cq_ref[rows, :])
        v_all = ci_ref[rows, :]
        for h in range(gh):
            cols = slice(h * HEAD_DIM, (h + 1) * HEAD_DIM)
            o, st = _hgrn_chunk(q_all[:, cols], k_all[:, cols], v_all[:, cols], b_all[:, cols],
                                state_ref[h])
            state_ref[h] = st
            of_ref[rows, cols] = o
        return 0

    lax.fori_loop(0, ts // c, chunk, 0)

    for h in range(gh):
        cols = slice(h * HEAD_DIM, (h + 1) * HEAD_DIM)
        o = _rms(of_ref[:, cols], gain_ref[:, cols]) * _silu(cg_ref[:, cols])
        o_ref[:, cols] = o.astype(BF16)


def _hgrn(proj, lower_bound, gains, layer, batch, seq, gh, col0, ts):
    gw = gh * HEAD_DIM
    nt = seq // ts
    blk = lambda k: pl.BlockSpec((ts, gw), lambda b, i, k=k: (b * nt + i, col0 + k))
    return pl.pallas_call(
        functools.partial(_hgrn_kernel, gh=gh),
        grid=(batch, nt),
        in_specs=[
            blk(0), blk(1), blk(2), blk(3),
            pl.BlockSpec((None, 1, gw), lambda b, i: (layer, 0, 0)),
            pl.BlockSpec((None, None, 1, gw), lambda b, i: (layer, 2, 0, 0)),
        ],
        out_specs=pl.BlockSpec((ts, gw), lambda b, i: (b * nt + i, 0)),
        out_shape=jax.ShapeDtypeStruct((batch * seq, gw), BF16),
        scratch_shapes=[pltpu.VMEM((gh, HEAD_DIM, HEAD_DIM), F32), pltpu.VMEM((ts, gw), F32)],
        compiler_params=pltpu.CompilerParams(
            dimension_semantics=("parallel", "arbitrary"), vmem_limit_bytes=VMEM_LIMIT),
        name="hgrn2",
    )(proj, proj, proj, proj, lower_bound, gains)


def _gdn_kernel(xq_ref, xk_ref, xv_ref, z_ref, ab_ref, wq_ref, wk_ref, wv_ref, alog_ref, dt_ref,
                gain_ref, o_ref, pad_ref, act_ref, state_ref, of_ref, *, gh):
    ts = xq_ref.shape[0]
    c = GDN_CHUNK
    first = pl.program_id(1) == 0

    @pl.when(first)
    def _():
        state_ref[...] = jnp.zeros_like(state_ref)
        pad_ref[:, 0:CONV_HALO, :] = jnp.zeros((3, CONV_HALO, pad_ref.shape[2]), F32)

    @pl.when(jnp.logical_not(first))
    def _():
        pad_ref[:, 0:CONV_HALO, :] = pad_ref[:, ts:ts + CONV_HALO, :]

    for part, (x_ref, w_ref) in enumerate(((xq_ref, wq_ref), (xk_ref, wk_ref), (xv_ref, wv_ref))):
        pad_ref[part, CONV_HALO:CONV_HALO + ts, :] = x_ref[...]
        acc = None
        for tap in range(CONV_WIDTH):
            start = CONV_HALO - (CONV_WIDTH - 1) + tap
            term = w_ref[tap:tap + 1, :] * pad_ref[part, start:start + ts, :]
            acc = term if acc is None else acc + term
        act_ref[part] = _silu(acc)

    ab = ab_ref[:, 0:HEAD_DIM]
    log_alpha = -jnp.exp(alog_ref[...]) * _softplus(ab + dt_ref[...])
    beta_all = _sigmoid(ab)

    ii = lax.broadcasted_iota(jnp.int32, (c, c), 0)
    jj = lax.broadcasted_iota(jnp.int32, (c, c), 1)
    incl = jj <= ii
    strict = jj < ii
    prefix = jnp.where(incl, 1.0, 0.0).astype(BF16)
    base_shift = int(np.log2(GDN_INV_BASE))
    in_base = jnp.where((ii >> base_shift) == (jj >> base_shift), 1.0, 0.0)
    merge_masks = []
    for shift in range(base_shift, int(np.log2(c))):
        same_pair = (ii >> (shift + 1)) == (jj >> (shift + 1))
        other_half = (ii >> shift) != (jj >> shift)
        merge_masks.append(jnp.where(same_pair & other_half, 1.0, 0.0))

    n_chunks = ts // c
    bodies = [(ci, h) for ci in range(n_chunks) for h in range(gh)]
    g_all, g_rows = [], []
    for ci in range(n_chunks):
        g = _dot_exact_lhs(prefix, log_alpha[ci * c:(ci + 1) * c])
        g_all.append(g)
        g_rows.append(g.T)

    def load(part, ci, h):
        return act_ref[part, ci * c:(ci + 1) * c, h * HEAD_DIM:(h + 1) * HEAD_DIM]

    def l2norm(t):
        return t * lax.rsqrt(jnp.sum(t * t, axis=-1, keepdims=True) + EPS)

    q = [l2norm(load(0, ci, h)) * (HEAD_DIM ** -0.5) for ci, h in bodies]
    k = [l2norm(load(1, ci, h)) for ci, h in bodies]
    v = [load(2, ci, h) for ci, h in bodies]
    beta = [beta_all[ci * c:(ci + 1) * c, gh + h:gh + h + 1] for ci, h in bodies]
    g_col = [g_all[ci][:, h:h + 1] for ci, h in bodies]
    g_last = [g_all[ci][c - 1:c, h:h + 1] for ci, h in bodies]
    decay = [jnp.where(incl, jnp.exp(jnp.minimum(gc - g_rows[ci][h:h + 1, :], 0.0)), 0.0)
             for gc, (ci, h) in zip(g_col, bodies)]
    k_beta = [ki * bi for ki, bi in zip(k, beta)]
    k16 = [ki.astype(BF16) for ki in k]
    a = [jnp.where(strict, _dot_nt(kb.astype(BF16), kh) * dc, 0.0)
         for kb, kh, dc in zip(k_beta, k16, decay)]
    power = [ai * in_base for ai in a]
    n = [-p for p in power]
    for _ in range(base_shift - 1):
        power = [_dot16(p, p) for p in power]
        n = [ni + p + _dot16(ni, p) for ni, p in zip(n, power)]
    for mask in merge_masks:
        off = [ai * mask for ai in a]
        x = [o + _dot16(ni, o) for ni, o in zip(n, off)]
        n = [ni - (xi + _dot16(xi, ni)) for ni, xi in zip(n, x)]
    rhs = [jnp.concatenate([vi * bi, kb * jnp.exp(gc)], axis=1)
           for vi, bi, kb, gc in zip(v, beta, k_beta, g_col)]
    uw = [r + _dot16(ni, r) for ni, r in zip(n, rhs)]
    qk = [(_dot_nt(qi.astype(BF16), kh) * dc).astype(BF16) for qi, kh, dc in zip(q, k16, decay)]
    q_in = [(qi * jnp.exp(gc)).astype(BF16) for qi, gc in zip(q, g_col)]
    k_out = [(ki * jnp.exp(gl - gc)).astype(BF16) for ki, gl, gc in zip(k, g_last, g_col)]

    for i, (ci, h) in enumerate(bodies):
        state = state_ref[h]
        s16 = state.astype(BF16)
        v_new = uw[i][:, :HEAD_DIM] - _dot(uw[i][:, HEAD_DIM:].astype(BF16), s16)
        vn16 = v_new.astype(BF16)
        o = _dot(q_in[i], s16) + _dot(qk[i], vn16)
        state_ref[h] = jnp.exp(g_last[i]) * state + _dot_tn(k_out[i], vn16)
        of_ref[ci * c:(ci + 1) * c, h * HEAD_DIM:(h + 1) * HEAD_DIM] = o

    for h in range(gh):
        cols = slice(h * HEAD_DIM, (h + 1) * HEAD_DIM)
        o = _rms(of_ref[:, cols], gain_ref[:, cols]) * _silu(z_ref[:, cols])
        o_ref[:, cols] = o.astype(BF16)


def _gdn(proj, conv_w, alog_row, dt_row, gains, layer, batch, seq, gh, col0, ts):
    gw = gh * HEAD_DIM
    nt = seq // ts
    blk = lambda k: pl.BlockSpec((ts, gw), lambda b, i, k=k: (b * nt + i, col0 + k))
    wblk = lambda k: pl.BlockSpec((None, CONV_WIDTH, gw), lambda b, i, k=k: (layer, 0, k))
    row = pl.BlockSpec((None, 1, HEAD_DIM), lambda b, i: (layer, 0, 0))
    return pl.pallas_call(
        functools.partial(_gdn_kernel, gh=gh),
        grid=(batch, nt),
        in_specs=[
            blk(0), blk(1), blk(2), blk(3), blk(4),
            wblk(0), wblk(1), wblk(2), row, row,
            pl.BlockSpec((None, None, 1, gw), lambda b, i: (layer, 3, 0, 0)),
        ],
        out_specs=pl.BlockSpec((ts, gw), lambda b, i: (b * nt + i, 0)),
        out_shape=jax.ShapeDtypeStruct((batch * seq, gw), BF16),
        scratch_shapes=[
            pltpu.VMEM((3, ts + CONV_HALO, gw), F32),
            pltpu.VMEM((3, ts, gw), F32),
            pltpu.VMEM((gh, HEAD_DIM, HEAD_DIM), F32),
            pltpu.VMEM((ts, gw), F32),
        ],
        compiler_params=pltpu.CompilerParams(
            dimension_semantics=("parallel", "arbitrary"), vmem_limit_bytes=VMEM_LIMIT),
        name="gated_deltanet",
    )(proj, proj, proj, proj, proj, conv_w, conv_w, conv_w, alog_row, dt_row, gains)


def _outproj_kernel(x_ref, oa_ref, ob_ref, oc_ref, od_ref, w_ref, gt_ref, gain_ref, o_ref):
    gw = oa_ref.shape[1]
    y = None
    for g, m_ref in enumerate((oa_ref, ob_ref, oc_ref, od_ref)):
        part = _dot(m_ref[...], w_ref[g * gw:(g + 1) * gw, :])
        y = part if y is None else y + part
    o_ref[...] = x_ref[...] + gt_ref[...] * _rms(y, gain_ref[...])


def _outproj(x, groups, w_out, gate, gain, layer, tiles_per_batch, tm):
    n, d = x.shape
    gw = groups[0].shape[1]
    gspec = pl.BlockSpec((tm, gw), lambda i: (i, 0))
    return pl.pallas_call(
        _outproj_kernel,
        grid=(n // tm,),
        in_specs=[
            pl.BlockSpec((tm, d), lambda i: (i, 0)),
            gspec, gspec, gspec, gspec,
            pl.BlockSpec((None, N_GROUPS * gw, d), lambda i: (layer, 0, 0)),
            pl.BlockSpec((None, 1, d), lambda i: (i // tiles_per_batch, 0, 0)),
            pl.BlockSpec((1, d), lambda i: (0, 0)),
        ],
        out_specs=pl.BlockSpec((tm, d), lambda i: (i, 0)),
        out_shape=jax.ShapeDtypeStruct((n, d), F32),
        compiler_params=pltpu.CompilerParams(
            dimension_semantics=("parallel",), vmem_limit_bytes=VMEM_LIMIT),
        name="outproj",
    )(x, *groups, w_out, gate, gain)


def _largest_tile(n, cap):
    t = min(n, cap)
    while n % t:
        t //= 2
    return t


def kernel(x, c, w_mod, b_mod, norm_gain, w_in, w_out, mix_out_gain, hgrn_lb_logits, dn_conv_w,
           dn_a_log, dn_dt_bias, ffn1_w13, ffn1_w2, ffn2_w13, ffn2_w2):
    batch, seq, d = x.shape
    depth = w_mod.shape[0]
    gw = d // N_GROUPS
    gh = gw // HEAD_DIM
    dff = ffn1_w2.shape[1]
    n = batch * seq

    tm_ffn = _largest_tile(seq, 512)
    tf = dff // 11 if dff % 11 == 0 and (dff // 11) % 128 == 0 else _largest_tile(dff, 512)
    tm_proj = _largest_tile(seq, 1024)
    tm_out = _largest_tile(seq, 512)
    tq = _largest_tile(seq, 256)
    ts_c = _largest_tile(seq, 1024)
    ts_d = _largest_tile(seq, 512)

    in_cols = w_in.shape[2]
    pad_cols = (-in_cols) % gw
    w_in16 = jnp.pad(w_in, ((0, 0), (0, 0), (0, pad_cols))).astype(BF16)
    w_out16 = w_out.astype(BF16)
    ffn_w = [(ffn1_w13.astype(BF16), ffn1_w2.astype(BF16)),
             (ffn2_w13.astype(BF16), ffn2_w2.astype(BF16))]
    lb_p = jax.nn.softmax(hgrn_lb_logits.astype(F32), axis=0)
    lower_bounds = (jnp.cumsum(lb_p, axis=0) - lb_p[0]).reshape(depth, 1, gw)
    gains = mix_out_gain.reshape(depth, N_GROUPS, 1, gw)
    head_gains = mix_out_gain.reshape(depth, N_GROUPS * gh, 1, HEAD_DIM)
    lane_pad = ((0, 0), (0, HEAD_DIM - gh))
    alog_row = jnp.pad(dn_a_log.astype(F32), lane_pad).reshape(depth, 1, HEAD_DIM)
    dt_row = jnp.pad(dn_dt_bias.astype(F32), lane_pad).reshape(depth, 1, HEAD_DIM)
    conv_w = dn_conv_w.astype(F32)

    c_rows = -(-batch // 8) * 8
    c_pad = jnp.pad(c, ((0, c_rows - batch), (0, 0)))
    mod = _modulation(c_pad, w_mod, b_mod, _largest_tile(N_MOD * d, 1024))[:, :batch]
    mod = mod.reshape(depth, batch, N_MOD, 1, d)

    xf = x.reshape(n, d)
    for l in range(depth):
        sh1, sc1, g1, sh2, sc2, g2, sh3, sc3, g3 = (mod[l, :, i] for i in range(N_MOD))
        ng = norm_gain[l].reshape(6, 1, d)
        xf = _ffn(xf, sh1, sc1, g1, ng[0], ng[1], *ffn_w[0], l, seq // tm_ffn, tm_ffn, tf)
        proj = _inproj(xf, sh2, sc2, ng[2], w_in16, l, seq // tm_proj, tm_proj, gw)
        hb = gw // HEAD_DIM
        o_a = _attn_a(proj, head_gains, l, batch, seq, gh, 0 * hb, tq, tq)
        o_b = _attn_b(proj, head_gains, l, batch, seq, gh, 3 * hb, tq)
        o_c = _hgrn(proj, lower_bounds, gains, l, batch, seq, gh, 6, ts_c)
        o_d = _gdn(proj, conv_w, alog_row, dt_row, gains, l, batch, seq, gh, 10, ts_d)
        xf = _outproj(xf, (o_a, o_b, o_c, o_d), w_out16, g2, ng[3], l, seq // tm_out, tm_out)
        xf = _ffn(xf, sh3, sc3, g3, ng[4], ng[5], *ffn_w[1], l, seq // tm_ffn, tm_ffn, tf)
    return xf.reshape(batch, seq, d)
```

```python
import functools

import jax
import jax.numpy as jnp
import numpy as np
from jax import lax
from jax.experimental import pallas as pl
from jax.experimental.pallas import tpu as pltpu

F32 = jnp.float32
BF16 = jnp.bfloat16

HEAD_DIM = 128
N_GROUPS = 4
N_MOD = 9
CONV_WIDTH = 4
DILATED_BRANCHES = ((128, 1), (512, 4), (2048, 16))
DILATED_GROUP = 2
EPS = 1e-6
NEG_BIG = -1e30
LB_FLOOR = 1e-30

HGRN_CHUNK = 32
HGRN_SUB = 8
GDN_CHUNK = 128
GDN_INV_BASE = 8
CONV_HALO = 8
NORM_ROWS = 128
VMEM_LIMIT = 56 * 1024 * 1024

_NT = (((1,), (1,)), ((), ()))
_TN = (((0,), (0,)), ((), ()))


def _dot(a, b):
    return jnp.dot(a, b, preferred_element_type=F32)


def _dot_nt(a, b):
    return lax.dot_general(a, b, _NT, preferred_element_type=F32)


def _dot_tn(a, b):
    return lax.dot_general(a, b, _TN, preferred_element_type=F32)


def _split3(a):
    hi = a.astype(BF16)
    r = a - hi.astype(F32)
    mid = r.astype(BF16)
    lo = (r - mid.astype(F32)).astype(BF16)
    return hi, mid, lo


def _dot_exact_lhs(a_bf16, b):
    hi, mid, lo = _split3(b)
    return _dot(a_bf16, hi) + (_dot(a_bf16, mid) + _dot(a_bf16, lo))


def _dot16(a, b):
    return _dot(a.astype(BF16), b.astype(BF16))


def _sigmoid(x):
    return 1.0 / (1.0 + jnp.exp(-x))


def _silu(x):
    return x * _sigmoid(x)


def _softplus(x):
    return jnp.maximum(x, 0.0) + jnp.log1p(jnp.exp(-jnp.abs(x)))


def _log_sigmoid(x):
    return jnp.minimum(x, 0.0) - jnp.log1p(jnp.exp(-jnp.abs(x)))


def _rms(x, gain):
    return x * lax.rsqrt(jnp.mean(x * x, axis=-1, keepdims=True) + EPS) * gain


def _mod_kernel(c_ref, w_ref, b_ref, o_ref):
    cond = _silu(c_ref[...]).astype(BF16)
    o_ref[...] = _dot(cond, w_ref[...].astype(BF16)) + b_ref[...]


def _modulation(c_pad, w_mod, b_mod, tn):
    depth, d, n = w_mod.shape
    rows = c_pad.shape[0]
    return pl.pallas_call(
        _mod_kernel,
        grid=(depth, n // tn),
        in_specs=[
            pl.BlockSpec((rows, d), lambda l, j: (0, 0)),
            pl.BlockSpec((None, d, tn), lambda l, j: (l, 0, j)),
            pl.BlockSpec((None, 1, tn), lambda l, j: (l, 0, j)),
        ],
        out_specs=pl.BlockSpec((None, rows, tn), lambda l, j: (l, 0, j)),
        out_shape=jax.ShapeDtypeStruct((depth, rows, n), F32),
        compiler_params=pltpu.CompilerParams(
            dimension_semantics=("parallel", "parallel"), vmem_limit_bytes=VMEM_LIMIT),
        name="modulation",
    )(c_pad, w_mod, b_mod.reshape(depth, 1, n))


def _norm_modulate(x, gain, shift, scale):
    return (_rms(x, gain) * (1.0 + scale) + shift).astype(BF16)


def _for_row_slices(n_rows, body):
    def step(r, carry):
        body(pl.ds(pl.multiple_of(r * NORM_ROWS, NORM_ROWS), NORM_ROWS))
        return carry

    lax.fori_loop(0, n_rows // NORM_ROWS, step, 0)


def _inv_rms(x):
    return lax.rsqrt(jnp.mean(x * x, axis=-1, keepdims=True) + EPS)


def _norm_modulate_tile(x_ref, r_ref, ga_ref, sh_ref, sc_ref, h_ref, also=None):
    r_ref[...] = _inv_rms(x_ref[...])

    def body(rows):
        y = x_ref[rows, :] * r_ref[rows, :] * ga_ref[...]
        h_ref[rows, :] = (y * (1.0 + sc_ref[...]) + sh_ref[...]).astype(BF16)
        if also is not None:
            also(rows)

    _for_row_slices(x_ref.shape[0], body)


def _ffn_kernel(x_ref, sh_ref, sc_ref, gt_ref, ga_ref, gb_ref, w1_ref, w3_ref, w2_ref,
                o_ref, h_ref, acc_ref, r_ref):
    j = pl.program_id(1)

    @pl.when(j == 0)
    def _():
        def clear(rows):
            acc_ref[rows, :] = jnp.zeros((NORM_ROWS, acc_ref.shape[1]), F32)

        _norm_modulate_tile(x_ref, r_ref, ga_ref, sh_ref, sc_ref, h_ref, also=clear)

    h = h_ref[...]
    gate = _dot(h, w1_ref[...])
    up = _dot(h, w3_ref[...])
    act = (_silu(gate) * up).astype(BF16)
    acc_ref[...] += _dot(act, w2_ref[...])

    @pl.when(j == pl.num_programs(1) - 1)
    def _():
        r_ref[...] = _inv_rms(acc_ref[...])

        def body(rows):
            y = acc_ref[rows, :] * r_ref[rows, :] * gb_ref[...]
            o_ref[rows, :] = x_ref[rows, :] + (0.5 * gt_ref[...]) * y

        _for_row_slices(x_ref.shape[0], body)


def _ffn(x, shift, scale, gate, gain_a, gain_b, w13, w2, layer, tiles_per_batch, tm, tf):
    n, d = x.shape
    dff = w2.shape[1]
    nf = dff // tf
    row = lambda i, j: (i, 0)
    per_batch = lambda i, j: (i // tiles_per_batch, 0, 0)
    const = lambda i, j: (0, 0)
    return pl.pallas_call(
        _ffn_kernel,
        grid=(n // tm, nf),
        in_specs=[
            pl.BlockSpec((tm, d), row),
            pl.BlockSpec((None, 1, d), per_batch),
            pl.BlockSpec((None, 1, d), per_batch),
            pl.BlockSpec((None, 1, d), per_batch),
            pl.BlockSpec((1, d), const),
            pl.BlockSpec((1, d), const),
            pl.BlockSpec((None, d, tf), lambda i, j: (layer, 0, j)),
            pl.BlockSpec((None, d, tf), lambda i, j: (layer, 0, nf + j)),
            pl.BlockSpec((None, tf, d), lambda i, j: (layer, j, 0)),
        ],
        out_specs=pl.BlockSpec((tm, d), row),
        out_shape=jax.ShapeDtypeStruct((n, d), F32),
        scratch_shapes=[pltpu.VMEM((tm, d), BF16), pltpu.VMEM((tm, d), F32),
                        pltpu.VMEM((tm, 1), F32)],
        compiler_params=pltpu.CompilerParams(
            dimension_semantics=("parallel", "arbitrary"), vmem_limit_bytes=VMEM_LIMIT),
        name="ffn",
    )(x, shift, scale, gate, gain_a, gain_b, w13, w13, w2)


def _inproj_kernel(x_ref, sh_ref, sc_ref, ga_ref, w_ref, o_ref, h_ref, r_ref):
    @pl.when(pl.program_id(1) == 0)
    def _():
        _norm_modulate_tile(x_ref, r_ref, ga_ref, sh_ref, sc_ref, h_ref)

    o_ref[...] = _dot(h_ref[...], w_ref[...])


def _inproj(x, shift, scale, gain, w_in, layer, tiles_per_batch, tm, tn):
    n, d = x.shape
    cols = w_in.shape[2]
    per_batch = lambda i, j: (i // tiles_per_batch, 0, 0)
    return pl.pallas_call(
        _inproj_kernel,
        grid=(n // tm, cols // tn),
        in_specs=[
            pl.BlockSpec((tm, d), lambda i, j: (i, 0)),
            pl.BlockSpec((None, 1, d), per_batch),
            pl.BlockSpec((None, 1, d), per_batch),
            pl.BlockSpec((1, d), lambda i, j: (0, 0)),
            pl.BlockSpec((None, d, tn), lambda i, j: (layer, 0, j)),
        ],
        out_specs=pl.BlockSpec((tm, tn), lambda i, j: (i, j)),
        out_shape=jax.ShapeDtypeStruct((n, cols), F32),
        scratch_shapes=[pltpu.VMEM((tm, d), BF16), pltpu.VMEM((tm, 1), F32)],
        compiler_params=pltpu.CompilerParams(
            dimension_semantics=("parallel", "arbitrary"), vmem_limit_bytes=VMEM_LIMIT),
        name="inproj",
    )(x, shift, scale, gain, w_in)


def _dilated_log_counts(tq, tk):
    reach = max(w for w, _ in DILATED_BRANCHES)
    nd = (reach + tq - 1) // tk + 1
    d = np.arange(nd)[:, None, None]
    i = np.arange(tq)[None, :, None]
    j = np.arange(tk)[None, None, :]
    delta = d * tk + i - j
    cnt = np.zeros(delta.shape, np.float64)
    for window, dil in DILATED_BRANCHES:
        cnt += (delta >= 0) & (delta <= window) & (delta % dil == 0)
    table = np.full((nd + 1, tq, tk), NEG_BIG, np.float32)
    table[:nd] = np.where(cnt > 0, np.log(np.maximum(cnt, 1.0)), NEG_BIG)
    return table


def _attn_a_kernel(q_ref, k_ref, v_ref, lc_ref, gain_ref, o_ref, kb_ref, vb_ref, s_ref, *, tq, tk, nd):
    qi = pl.program_id(2)

    @pl.when(qi == 0)
    def _():
        kb_ref[...] = k_ref[...].astype(BF16)
        vb_ref[:, 0:HEAD_DIM] = v_ref[...].astype(BF16)
        vb_ref[:, HEAD_DIM:] = jnp.ones((vb_ref.shape[0], HEAD_DIM), BF16)

    q = q_ref[...].astype(BF16)
    scale = HEAD_DIM ** -0.5
    half = tk // 2

    maxes, accs = [], []
    for g0 in range(0, nd, DILATED_GROUP):
        group = range(g0, min(g0 + DILATED_GROUP, nd))
        m_run = jnp.full((tq, half), NEG_BIG, F32)
        offsets = {}
        for d in group:
            c = qi - d
            offsets[d] = pl.multiple_of(jnp.maximum(c, 0) * tk, tk)
            t = (_dot_nt(q, kb_ref[pl.ds(offsets[d], tk), :]) * scale
                 + lc_ref[jnp.where(c >= 0, d, nd)])
            s_ref[d] = t
            m_run = jnp.maximum(m_run, jnp.maximum(t[:, :half], t[:, half:]))
        m = jnp.max(m_run, axis=-1, keepdims=True)
        acc = jnp.zeros((tq, 2 * HEAD_DIM), F32)
        for d in group:
            p = jnp.exp(s_ref[d] - m)
            acc = acc + _dot(p.astype(BF16), vb_ref[pl.ds(offsets[d], tk), :])
        maxes.append(m)
        accs.append(acc)
    m_all = functools.reduce(jnp.maximum, maxes)
    acc = functools.reduce(lambda a, b: a + b,
                           [jnp.exp(m - m_all) * a for m, a in zip(maxes, accs)])
    o = acc[:, :HEAD_DIM] / acc[:, HEAD_DIM:]
    o_ref[...] = _rms(o, gain_ref[...]).astype(BF16)


def _attn_a(proj, gains, layer, batch, seq, gh, col0, tq, tk):
    table = jnp.asarray(_dilated_log_counts(tq, tk))
    nd = table.shape[0] - 1
    nq = seq // tq
    return pl.pallas_call(
        functools.partial(_attn_a_kernel, tq=tq, tk=tk, nd=nd),
        grid=(batch, gh, nq),
        in_specs=[
            pl.BlockSpec((tq, HEAD_DIM), lambda b, h, i: (b * nq + i, col0 + h)),
            pl.BlockSpec((seq, HEAD_DIM), lambda b, h, i: (b, col0 + gh + h)),
            pl.BlockSpec((seq, HEAD_DIM), lambda b, h, i: (b, col0 + 2 * gh + h)),
            pl.BlockSpec((nd + 1, tq, tk), lambda b, h, i: (0, 0, 0)),
            pl.BlockSpec((None, None, 1, HEAD_DIM), lambda b, h, i: (layer, h, 0, 0)),
        ],
        out_specs=pl.BlockSpec((tq, HEAD_DIM), lambda b, h, i: (b * nq + i, h)),
        out_shape=jax.ShapeDtypeStruct((batch * seq, gh * HEAD_DIM), BF16),
        scratch_shapes=[pltpu.VMEM((seq, HEAD_DIM), BF16), pltpu.VMEM((seq, 2 * HEAD_DIM), BF16),
                        pltpu.VMEM((nd, tq, tk), F32)],
        compiler_params=pltpu.CompilerParams(
            dimension_semantics=("parallel", "parallel", "arbitrary"),
            vmem_limit_bytes=VMEM_LIMIT),
        name="dilated_attention",
    )(proj, proj, proj, table, gains)


STICK_UNDERFLOW = 100.0
STICK_BOUND_SLACK = 1.01


def _attn_b_kernel(q_ref, k_ref, v_ref, gain_ref, o_ref, kb_ref, vb_ref, kmax_ref, *, tq):
    qi = pl.program_id(2)
    tk = tq

    @pl.when(qi == 0)
    def _():
        k = k_ref[...]
        kb_ref[...] = k.astype(BF16)
        vb_ref[...] = v_ref[...].astype(BF16)
        k_sq = jnp.max(jnp.sum(k * k, axis=-1, keepdims=True), axis=0, keepdims=True)
        kmax_ref[...] = jnp.broadcast_to(jnp.sqrt(k_sq), kmax_ref.shape)

    qf = q_ref[...]
    q = qf.astype(BF16)
    scale = HEAD_DIM ** -0.5
    q_norm = jnp.sqrt(jnp.sum(qf * qf, axis=-1, keepdims=True))
    z_bound = (scale * STICK_BOUND_SLACK) * q_norm * kmax_ref[0:1, 0:1]
    jj = lax.broadcasted_iota(jnp.int32, (tk, tk), 0)
    ss = lax.broadcasted_iota(jnp.int32, (tk, tk), 1)
    suffix = jnp.where(jj >= ss, 1.0, 0.0).astype(BF16)

    def logits(c, keep):
        off = pl.multiple_of(c * tk, tk)
        z = _dot_nt(q, kb_ref[pl.ds(off, tk), :]) * scale
        log_keep = _log_sigmoid(-z)
        if keep is not None:
            log_keep = jnp.where(keep, log_keep, 0.0)
        return z, log_keep, vb_ref[pl.ds(off, tk), :]

    def suffix_sums(log_keep):
        lk_hi = log_keep.astype(BF16)
        lk_lo = (log_keep - lk_hi.astype(F32)).astype(BF16)
        return _dot(lk_hi, suffix) + _dot(lk_lo, suffix)

    def live(tail):
        return (jnp.max(tail + z_bound) > -STICK_UNDERFLOW).astype(jnp.int32)

    causal = ss < jj
    has_left = qi > 0
    z_d, lk_d, v_d = logits(qi, causal)
    z_l, lk_l, v_l = logits(jnp.maximum(qi - 1, 0), has_left)
    within_d = suffix_sums(lk_d)
    within_l = suffix_sums(lk_l) + within_d[:, 0:1]
    a_d = jnp.where(causal, jnp.exp(z_d + within_d), 0.0)
    a_l = jnp.where(has_left, jnp.exp(z_l + within_l), 0.0)
    acc = _dot(a_d.astype(BF16), v_d) + _dot(a_l.astype(BF16), v_l)
    tail = within_l[:, 0:1]

    def cond(carry):
        c, go, _, _ = carry
        return jnp.logical_and(c >= 0, go > 0)

    def body(carry):
        c, _, tail, acc = carry
        z, log_keep, v = logits(c, None)
        within = suffix_sums(log_keep) + tail
        acc = acc + _dot(jnp.exp(z + within).astype(BF16), v)
        tail = within[:, 0:1]
        return c - 1, live(tail), tail, acc

    _, _, _, acc = lax.while_loop(cond, body, (qi - 2, live(tail), tail, acc))
    o_ref[...] = _rms(acc, gain_ref[...]).astype(BF16)


def _attn_b(proj, gains, layer, batch, seq, gh, col0, tq):
    nq = seq // tq
    return pl.pallas_call(
        functools.partial(_attn_b_kernel, tq=tq),
        grid=(batch, gh, nq),
        in_specs=[
            pl.BlockSpec((tq, HEAD_DIM), lambda b, h, i: (b * nq + i, col0 + h)),
            pl.BlockSpec((seq, HEAD_DIM), lambda b, h, i: (b, col0 + gh + h)),
            pl.BlockSpec((seq, HEAD_DIM), lambda b, h, i: (b, col0 + 2 * gh + h)),
            pl.BlockSpec((None, None, 1, HEAD_DIM), lambda b, h, i: (layer, gh + h, 0, 0)),
        ],
        out_specs=pl.BlockSpec((tq, HEAD_DIM), lambda b, h, i: (b * nq + i, h)),
        out_shape=jax.ShapeDtypeStruct((batch * seq, gh * HEAD_DIM), BF16),
        scratch_shapes=[pltpu.VMEM((seq, HEAD_DIM), BF16), pltpu.VMEM((seq, HEAD_DIM), BF16),
                        pltpu.VMEM((8, HEAD_DIM), F32)],
        compiler_params=pltpu.CompilerParams(
            dimension_semantics=("parallel", "parallel", "arbitrary"),
            vmem_limit_bytes=VMEM_LIMIT),
        name="stick_breaking_attention",
    )(proj, proj, proj, gains)


def _hgrn_chunk(q, k, v, b, state_t):
    c = HGRN_CHUNK
    nsub = c // HGRN_SUB
    o_inter = _dot_nt((q * jnp.exp(b)).astype(BF16), state_t.astype(BF16))
    row_id = lax.broadcasted_iota(jnp.int32, (HGRN_SUB, 1), 0)
    blocks = [o_inter[HGRN_SUB * i:HGRN_SUB * (i + 1)] for i in range(nsub)]
    for sub in range(nsub):
        lo = sub * HGRN_SUB
        hi = lo + HGRN_SUB
        q_d, b_d = q[lo:hi], b[lo:hi]
        q_r, b_r = q[hi:], b[hi:]
        diag = None
        rest = None
        for si in range(HGRN_SUB):
            s = lo + si
            b_s, k_s, v_s = b[s:s + 1], k[s:s + 1], v[s:s + 1]
            decay = jnp.exp(jnp.minimum(b_d - b_s, 0.0))
            score = jnp.sum(q_d * decay * k_s, axis=-1, keepdims=True)
            if si > 0:
                score = jnp.where(row_id >= si, score, 0.0)
            term = score * v_s
            diag = term if diag is None else diag + term
            if sub < nsub - 1:
                score = jnp.sum(q_r * jnp.exp(b_r - b_s) * k_s, axis=-1, keepdims=True)
                term = score * v_s
                rest = term if rest is None else rest + term
        blocks[sub] = blocks[sub] + diag
        for i in range(sub + 1, nsub):
            r = (i - sub - 1) * HGRN_SUB
            blocks[i] = blocks[i] + rest[r:r + HGRN_SUB]
    o = jnp.concatenate(blocks, axis=0)
    b_last = b[c - 1:c]
    k_decayed = k * jnp.exp(b_last - b)
    new_state_t = jnp.exp(b_last) * state_t + _dot_tn(v.astype(BF16), k_decayed.astype(BF16))
    return o, new_state_t


def _hgrn_kernel(cq_ref, cf_ref, ci_ref, cg_ref, lb_ref, gain_ref, o_ref, state_ref, of_ref, *, gh):
    ts = cq_ref.shape[0]
    c = HGRN_CHUNK

    @pl.when(pl.program_id(1) == 0)
    def _():
        state_ref[...] = jnp.zeros_like(state_ref)

    lb = lb_ref[...]
    log_lb = jnp.log(jnp.maximum(lb, LB_FLOOR))
    log_1m_lb = jnp.log1p(-lb)
    ii = lax.broadcasted_iota(jnp.int32, (c, c), 0)
    jj = lax.broadcasted_iota(jnp.int32, (c, c), 1)
    prefix = jnp.where(jj <= ii, 1.0, 0.0).astype(BF16)

    def chunk(ci, _):
        base = pl.multiple_of(ci * c, c)
        rows = pl.ds(base, c)
        cf = cf_ref[rows, :]
        x = log_1m_lb + _log_sigmoid(cf)
        hi = jnp.maximum(log_lb, x)
        log_f = hi + jnp.log1p(jnp.exp(-jnp.abs(log_lb - x)))
        b_all = _dot_exact_lhs(prefix, log_f)
        k_all = (1.0 - lb) * _sigmoid(-cf)
        q_all = _silu(cq_ref[rows, :])
        v_all = ci_ref[rows, :]
        for h in range(gh):
            cols = slice(h * HEAD_DIM, (h + 1) * HEAD_DIM)
            o, st = _hgrn_chunk(q_all[:, cols], k_all[:, cols], v_all[:, cols], b_all[:, cols],
                                state_ref[h])
            state_ref[h] = st
            of_ref[rows, cols] = o
        return 0

    lax.fori_loop(0, ts // c, chunk, 0, unroll=2)

    for h in range(gh):
        cols = slice(h * HEAD_DIM, (h + 1) * HEAD_DIM)
        o = _rms(of_ref[:, cols], gain_ref[:, cols]) * _silu(cg_ref[:, cols])
        o_ref[:, cols] = o.astype(BF16)


def _hgrn(proj, lower_bound, gains, layer, batch, seq, gh, col0, ts):
    gw = gh * HEAD_DIM
    nt = seq // ts
    blk = lambda k: pl.BlockSpec((ts, gw), lambda b, i, k=k: (b * nt + i, col0 + k))
    return pl.pallas_call(
        functools.partial(_hgrn_kernel, gh=gh),
        grid=(batch, nt),
        in_specs=[
            blk(0), blk(1), blk(2), blk(3),
            pl.BlockSpec((None, 1, gw), lambda b, i: (layer, 0, 0)),
            pl.BlockSpec((None, None, 1, gw), lambda b, i: (layer, 2, 0, 0)),
        ],
        out_specs=pl.BlockSpec((ts, gw), lambda b, i: (b * nt + i, 0)),
        out_shape=jax.ShapeDtypeStruct((batch * seq, gw), BF16),
        scratch_shapes=[pltpu.VMEM((gh, HEAD_DIM, HEAD_DIM), F32), pltpu.VMEM((ts, gw), F32)],
        compiler_params=pltpu.CompilerParams(
            dimension_semantics=("parallel", "arbitrary"), vmem_limit_bytes=VMEM_LIMIT),
        name="hgrn2",
    )(proj, proj, proj, proj, lower_bound, gains)


def _gdn_kernel(xq_ref, xk_ref, xv_ref, z_ref, ab_ref, wq_ref, wk_ref, wv_ref, alog_ref, dt_ref,
                gain_ref, o_ref, pad_ref, act_ref, state_ref, of_ref, *, gh):
    ts = xq_ref.shape[0]
    c = GDN_CHUNK
    first = pl.program_id(1) == 0

    @pl.when(first)
    def _():
        state_ref[...] = jnp.zeros_like(state_ref)
        pad_ref[:, 0:CONV_HALO, :] = jnp.zeros((3, CONV_HALO, pad_ref.shape[2]), F32)

    @pl.when(jnp.logical_not(first))
    def _():
        pad_ref[:, 0:CONV_HALO, :] = pad_ref[:, ts:ts + CONV_HALO, :]

    for part, (x_ref, w_ref) in enumerate(((xq_ref, wq_ref), (xk_ref, wk_ref), (xv_ref, wv_ref))):
        pad_ref[part, CONV_HALO:CONV_HALO + ts, :] = x_ref[...]
        acc = None
        for tap in range(CONV_WIDTH):
            start = CONV_HALO - (CONV_WIDTH - 1) + tap
            term = w_ref[tap:tap + 1, :] * pad_ref[part, start:start + ts, :]
            acc = term if acc is None else acc + term
        act_ref[part] = _silu(acc)

    ab = ab_ref[:, 0:HEAD_DIM]
    log_alpha = -jnp.exp(alog_ref[...]) * _softplus(ab + dt_ref[...])
    beta_all = _sigmoid(ab)

    ii = lax.broadcasted_iota(jnp.int32, (c, c), 0)
    jj = lax.broadcasted_iota(jnp.int32, (c, c), 1)
    incl = jj <= ii
    strict = jj < ii
    prefix = jnp.where(incl, 1.0, 0.0).astype(BF16)
    base_shift = int(np.log2(GDN_INV_BASE))
    in_base = jnp.where((ii >> base_shift) == (jj >> base_shift), 1.0, 0.0)
    merge_masks = []
    for shift in range(base_shift, int(np.log2(c))):
        same_pair = (ii >> (shift + 1)) == (jj >> (shift + 1))
        other_half = (ii >> shift) != (jj >> shift)
        merge_masks.append(jnp.where(same_pair & other_half, 1.0, 0.0))

    n_chunks = ts // c
    bodies = [(ci, h) for ci in range(n_chunks) for h in range(gh)]
    g_all, g_rows = [], []
    for ci in range(n_chunks):
        g = _dot_exact_lhs(prefix, log_alpha[ci * c:(ci + 1) * c])
        g_all.append(g)
        g_rows.append(g.T)

    def load(part, ci, h):
        return act_ref[part, ci * c:(ci + 1) * c, h * HEAD_DIM:(h + 1) * HEAD_DIM]

    def l2norm(t):
        return t * lax.rsqrt(jnp.sum(t * t, axis=-1, keepdims=True) + EPS)

    q = [l2norm(load(0, ci, h)) * (HEAD_DIM ** -0.5) for ci, h in bodies]
    k = [l2norm(load(1, ci, h)) for ci, h in bodies]
    v = [load(2, ci, h) for ci, h in bodies]
    beta = [beta_all[ci * c:(ci + 1) * c, gh + h:gh + h + 1] for ci, h in bodies]
    g_col = [g_all[ci][:, h:h + 1] for ci, h in bodies]
    g_last = [g_all[ci][c - 1:c, h:h + 1] for ci, h in bodies]
    decay = [jnp.where(incl, jnp.exp(jnp.minimum(gc - g_rows[ci][h:h + 1, :], 0.0)), 0.0)
             for gc, (ci, h) in zip(g_col, bodies)]
    k_beta = [ki * bi for ki, bi in zip(k, beta)]
    k16 = [ki.astype(BF16) for ki in k]
    a = [jnp.where(strict, _dot_nt(kb.astype(BF16), kh) * dc, 0.0)
         for kb, kh, dc in zip(k_beta, k16, decay)]
    power = [ai * in_base for ai in a]
    n = [-p for p in power]
    for _ in range(base_shift - 1):
        power = [_dot16(p, p) for p in power]
        n = [ni + p + _dot16(ni, p) for ni, p in zip(n, power)]
    for mask in merge_masks:
        off = [ai * mask for ai in a]
        x = [o + _dot16(ni, o) for ni, o in zip(n, off)]
        n = [ni - (xi + _dot16(xi, ni)) for ni, xi in zip(n, x)]
    rhs = [jnp.concatenate([vi * bi, kb * jnp.exp(gc)], axis=1)
           for vi, bi, kb, gc in zip(v, beta, k_beta, g_col)]
    uw = [r + _dot16(ni, r) for ni, r in zip(n, rhs)]
    qk = [(_dot_nt(qi.astype(BF16), kh) * dc).astype(BF16) for qi, kh, dc in zip(q, k16, decay)]
    q_in = [(qi * jnp.exp(gc)).astype(BF16) for qi, gc in zip(q, g_col)]
    k_out = [(ki * jnp.exp(gl - gc)).astype(BF16) for ki, gl, gc in zip(k, g_last, g_col)]

    for i, (ci, h) in enumerate(bodies):
        state = state_ref[h]
        s16 = state.astype(BF16)
        v_new = uw[i][:, :HEAD_DIM] - _dot(uw[i][:, HEAD_DIM:].astype(BF16), s16)
        vn16 = v_new.astype(BF16)
        o = _dot(q_in[i], s16) + _dot(qk[i], vn16)
        state_ref[h] = jnp.exp(g_last[i]) * state + _dot_tn(k_out[i], vn16)
        of_ref[ci * c:(ci + 1) * c, h * HEAD_DIM:(h + 1) * HEAD_DIM] = o

    for h in range(gh):
        cols = slice(h * HEAD_DIM, (h + 1) * HEAD_DIM)
        o = _rms(of_ref[:, cols], gain_ref[:, cols]) * _silu(z_ref[:, cols])
        o_ref[:, cols] = o.astype(BF16)


def _gdn(proj, conv_w, alog_row, dt_row, gains, layer, batch, seq, gh, col0, ts):
    gw = gh * HEAD_DIM
    nt = seq // ts
    blk = lambda k: pl.BlockSpec((ts, gw), lambda b, i, k=k: (b * nt + i, col0 + k))
    wblk = lambda k: pl.BlockSpec((None, CONV_WIDTH, gw), lambda b, i, k=k: (layer, 0, k))
    row = pl.BlockSpec((None, 1, HEAD_DIM), lambda b, i: (layer, 0, 0))
    return pl.pallas_call(
        functools.partial(_gdn_kernel, gh=gh),
        grid=(batch, nt),
        in_specs=[
            blk(0), blk(1), blk(2), blk(3), blk(4),
            wblk(0), wblk(1), wblk(2), row, row,
            pl.BlockSpec((None, None, 1, gw), lambda b, i: (layer, 3, 0, 0)),
        ],
        out_specs=pl.BlockSpec((ts, gw), lambda b, i: (b * nt + i, 0)),
        out_shape=jax.ShapeDtypeStruct((batch * seq, gw), BF16),
        scratch_shapes=[
            pltpu.VMEM((3, ts + CONV_HALO, gw), F32),
            pltpu.VMEM((3, ts, gw), F32),
            pltpu.VMEM((gh, HEAD_DIM, HEAD_DIM), F32),
            pltpu.VMEM((ts, gw), F32),
        ],
        compiler_params=pltpu.CompilerParams(
            dimension_semantics=("parallel", "arbitrary"), vmem_limit_bytes=VMEM_LIMIT),
        name="gated_deltanet",
    )(proj, proj, proj, proj, proj, conv_w, conv_w, conv_w, alog_row, dt_row, gains)


def _outproj_kernel(x_ref, oa_ref, ob_ref, oc_ref, od_ref, w_ref, gt_ref, gain_ref, o_ref,
                    y_ref, r_ref):
    gw = oa_ref.shape[1]
    y = None
    for g, m_ref in enumerate((oa_ref, ob_ref, oc_ref, od_ref)):
        part = _dot(m_ref[...], w_ref[g * gw:(g + 1) * gw, :])
        y = part if y is None else y + part
    y_ref[...] = y
    r_ref[...] = _inv_rms(y_ref[...])

    def body(rows):
        normed = y_ref[rows, :] * r_ref[rows, :] * gain_ref[...]
        o_ref[rows, :] = x_ref[rows, :] + gt_ref[...] * normed

    _for_row_slices(x_ref.shape[0], body)


def _outproj(x, groups, w_out, gate, gain, layer, tiles_per_batch, tm):
    n, d = x.shape
    gw = groups[0].shape[1]
    gspec = pl.BlockSpec((tm, gw), lambda i: (i, 0))
    return pl.pallas_call(
        _outproj_kernel,
        grid=(n // tm,),
        in_specs=[
            pl.BlockSpec((tm, d), lambda i: (i, 0)),
            gspec, gspec, gspec, gspec,
            pl.BlockSpec((None, N_GROUPS * gw, d), lambda i: (layer, 0, 0)),
            pl.BlockSpec((None, 1, d), lambda i: (i // tiles_per_batch, 0, 0)),
            pl.BlockSpec((1, d), lambda i: (0, 0)),
        ],
        out_specs=pl.BlockSpec((tm, d), lambda i: (i, 0)),
        out_shape=jax.ShapeDtypeStruct((n, d), F32),
        scratch_shapes=[pltpu.VMEM((tm, d), F32), pltpu.VMEM((tm, 1), F32)],
        compiler_params=pltpu.CompilerParams(
            dimension_semantics=("parallel",), vmem_limit_bytes=VMEM_LIMIT),
        name="outproj",
    )(x, *groups, w_out, gate, gain)


def _largest_tile(n, cap):
    t = min(n, cap)
    while n % t:
        t //= 2
    return t


def kernel(x, c, w_mod, b_mod, norm_gain, w_in, w_out, mix_out_gain, hgrn_lb_logits, dn_conv_w,
           dn_a_log, dn_dt_bias, ffn1_w13, ffn1_w2, ffn2_w13, ffn2_w2):
    batch, seq, d = x.shape
    depth = w_mod.shape[0]
    gw = d // N_GROUPS
    gh = gw // HEAD_DIM
    dff = ffn1_w2.shape[1]
    n = batch * seq

    tm_ffn = _largest_tile(seq, 512)
    tf = dff // 11 if dff % 11 == 0 and (dff // 11) % 128 == 0 else _largest_tile(dff, 512)
    tm_proj = _largest_tile(seq, 1024)
    tm_out = _largest_tile(seq, 512)
    tq = _largest_tile(seq, 256)
    ts_c = _largest_tile(seq, 1024)
    ts_d = _largest_tile(seq, 512)

    in_cols = w_in.shape[2]
    pad_cols = (-in_cols) % gw
    w_in16 = jnp.pad(w_in, ((0, 0), (0, 0), (0, pad_cols))).astype(BF16)
    w_out16 = w_out.astype(BF16)
    ffn_w = [(ffn1_w13.astype(BF16), ffn1_w2.astype(BF16)),
             (ffn2_w13.astype(BF16), ffn2_w2.astype(BF16))]
    lb_p = jax.nn.softmax(hgrn_lb_logits.astype(F32), axis=0)
    lower_bounds = (jnp.cumsum(lb_p, axis=0) - lb_p[0]).reshape(depth, 1, gw)
    gains = mix_out_gain.reshape(depth, N_GROUPS, 1, gw)
    head_gains = mix_out_gain.reshape(depth, N_GROUPS * gh, 1, HEAD_DIM)
    lane_pad = ((0, 0), (0, HEAD_DIM - gh))
    alog_row = jnp.pad(dn_a_log.astype(F32), lane_pad).reshape(depth, 1, HEAD_DIM)
    dt_row = jnp.pad(dn_dt_bias.astype(F32), lane_pad).reshape(depth, 1, HEAD_DIM)
    conv_w = dn_conv_w.astype(F32)

    c_rows = -(-batch // 8) * 8
    c_pad = jnp.pad(c, ((0, c_rows - batch), (0, 0)))
    mod = _modulation(c_pad, w_mod, b_mod, _largest_tile(N_MOD * d, 2048))[:, :batch]
    mod = mod.reshape(depth, batch, N_MOD, 1, d)

    xf = x.reshape(n, d)
    for l in range(depth):
        sh1, sc1, g1, sh2, sc2, g2, sh3, sc3, g3 = (mod[l, :, i] for i in range(N_MOD))
        ng = norm_gain[l].reshape(6, 1, d)
        xf = _ffn(xf, sh1, sc1, g1, ng[0], ng[1], *ffn_w[0], l, seq // tm_ffn, tm_ffn, tf)
        proj = _inproj(xf, sh2, sc2, ng[2], w_in16, l, seq // tm_proj, tm_proj, gw)
        hb = gw // HEAD_DIM
        o_a = _attn_a(proj, head_gains, l, batch, seq, gh, 0 * hb, tq, tq)
        o_b = _attn_b(proj, head_gains, l, batch, seq, gh, 3 * hb, tq)
        o_c = _hgrn(proj, lower_bounds, gains, l, batch, seq, gh, 6, ts_c)
        o_d = _gdn(proj, conv_w, alog_row, dt_row, gains, l, batch, seq, gh, 10, ts_d)
        xf = _outproj(xf, (o_a, o_b, o_c, o_d), w_out16, g2, ng[3], l, seq // tm_out, tm_out)
        xf = _ffn(xf, sh3, sc3, g3, ng[4], ng[5], *ffn_w[1], l, seq // tm_ffn, tm_ffn, tf)
    return xf.reshape(batch, seq, d)
```

```python
import functools

import jax
import jax.numpy as jnp
import numpy as np
from jax import lax
from jax.experimental import pallas as pl
from jax.experimental.pallas import tpu as pltpu

F32 = jnp.float32
BF16 = jnp.bfloat16

HEAD_DIM = 128
N_GROUPS = 4
N_MOD = 9
CONV_WIDTH = 4
DILATED_BRANCHES = ((128, 1), (512, 4), (2048, 16))
DILATED_GROUP = 2
EPS = 1e-6
NEG_BIG = -1e30
LB_FLOOR = 1e-30

HGRN_CHUNK = 32
HGRN_SUB = 8
GDN_CHUNK = 128
GDN_INV_BASE = 8
CONV_HALO = 8
NORM_ROWS = 128
VMEM_LIMIT = 56 * 1024 * 1024

_NT = (((1,), (1,)), ((), ()))
_TN = (((0,), (0,)), ((), ()))


def _dot(a, b):
    return jnp.dot(a, b, preferred_element_type=F32)


def _dot_nt(a, b):
    return lax.dot_general(a, b, _NT, preferred_element_type=F32)


def _dot_tn(a, b):
    return lax.dot_general(a, b, _TN, preferred_element_type=F32)


def _split3(a):
    hi = a.astype(BF16)
    r = a - hi.astype(F32)
    mid = r.astype(BF16)
    lo = (r - mid.astype(F32)).astype(BF16)
    return hi, mid, lo


def _dot_exact_lhs(a_bf16, b):
    hi, mid, lo = _split3(b)
    return _dot(a_bf16, hi) + (_dot(a_bf16, mid) + _dot(a_bf16, lo))


def _dot16(a, b):
    return _dot(a.astype(BF16), b.astype(BF16))


def _sigmoid(x):
    return 1.0 / (1.0 + jnp.exp(-x))


def _silu(x):
    return x * _sigmoid(x)


def _softplus(x):
    return jnp.maximum(x, 0.0) + jnp.log1p(jnp.exp(-jnp.abs(x)))


def _log_sigmoid(x):
    return jnp.minimum(x, 0.0) - jnp.log1p(jnp.exp(-jnp.abs(x)))


def _rms(x, gain):
    return x * lax.rsqrt(jnp.mean(x * x, axis=-1, keepdims=True) + EPS) * gain


def _mod_kernel(c_ref, w_ref, b_ref, o_ref):
    cond = _silu(c_ref[...]).astype(BF16)
    o_ref[...] = _dot(cond, w_ref[...].astype(BF16)) + b_ref[...]


def _modulation(c_pad, w_mod, b_mod, tn):
    depth, d, n = w_mod.shape
    rows = c_pad.shape[0]
    return pl.pallas_call(
        _mod_kernel,
        grid=(depth, n // tn),
        in_specs=[
            pl.BlockSpec((rows, d), lambda l, j: (0, 0)),
            pl.BlockSpec((None, d, tn), lambda l, j: (l, 0, j)),
            pl.BlockSpec((None, 1, tn), lambda l, j: (l, 0, j)),
        ],
        out_specs=pl.BlockSpec((None, rows, tn), lambda l, j: (l, 0, j)),
        out_shape=jax.ShapeDtypeStruct((depth, rows, n), F32),
        compiler_params=pltpu.CompilerParams(
            dimension_semantics=("parallel", "parallel"), vmem_limit_bytes=VMEM_LIMIT),
        name="modulation",
    )(c_pad, w_mod, b_mod.reshape(depth, 1, n))


def _norm_modulate(x, gain, shift, scale):
    return (_rms(x, gain) * (1.0 + scale) + shift).astype(BF16)


def _for_row_slices(n_rows, body):
    def step(r, carry):
        body(pl.ds(pl.multiple_of(r * NORM_ROWS, NORM_ROWS), NORM_ROWS))
        return carry

    lax.fori_loop(0, n_rows // NORM_ROWS, step, 0)


def _inv_rms(x):
    return lax.rsqrt(jnp.mean(x * x, axis=-1, keepdims=True) + EPS)


def _norm_modulate_tile(x_ref, r_ref, ga_ref, sh_ref, sc_ref, h_ref, also=None):
    r_ref[...] = _inv_rms(x_ref[...])

    def body(rows):
        y = x_ref[rows, :] * r_ref[rows, :] * ga_ref[...]
        h_ref[rows, :] = (y * (1.0 + sc_ref[...]) + sh_ref[...]).astype(BF16)
        if also is not None:
            also(rows)

    _for_row_slices(x_ref.shape[0], body)


def _ffn_kernel(x_ref, sh_ref, sc_ref, gt_ref, ga_ref, gb_ref, w1_ref, w3_ref, w2_ref,
                o_ref, h_ref, acc_ref, r_ref):
    j = pl.program_id(1)

    @pl.when(j == 0)
    def _():
        def clear(rows):
            acc_ref[rows, :] = jnp.zeros((NORM_ROWS, acc_ref.shape[1]), F32)

        _norm_modulate_tile(x_ref, r_ref, ga_ref, sh_ref, sc_ref, h_ref, also=clear)

    h = h_ref[...]
    gate = _dot(h, w1_ref[...])
    up = _dot(h, w3_ref[...])
    act = (_silu(gate) * up).astype(BF16)
    acc_ref[...] += _dot(act, w2_ref[...])

    @pl.when(j == pl.num_programs(1) - 1)
    def _():
        r_ref[...] = _inv_rms(acc_ref[...])

        def body(rows):
            y = acc_ref[rows, :] * r_ref[rows, :] * gb_ref[...]
            o_ref[rows, :] = x_ref[rows, :] + (0.5 * gt_ref[...]) * y

        _for_row_slices(x_ref.shape[0], body)


def _column_tiles(w, tn):
    depth, k, n = w.shape
    return w.reshape(depth, k, n // tn, tn).transpose(0, 2, 1, 3)


def _ffn(x, shift, scale, gate, gain_a, gain_b, w13, w2, layer, tiles_per_batch, tm):
    n, d = x.shape
    tf = w13.shape[3]
    nf = w13.shape[1] // 2
    row = lambda i, j: (i, 0)
    per_batch = lambda i, j: (i // tiles_per_batch, 0, 0)
    const = lambda i, j: (0, 0)
    return pl.pallas_call(
        _ffn_kernel,
        grid=(n // tm, nf),
        in_specs=[
            pl.BlockSpec((tm, d), row),
            pl.BlockSpec((None, 1, d), per_batch),
            pl.BlockSpec((None, 1, d), per_batch),
            pl.BlockSpec((None, 1, d), per_batch),
            pl.BlockSpec((1, d), const),
            pl.BlockSpec((1, d), const),
            pl.BlockSpec((None, None, d, tf), lambda i, j: (layer, j, 0, 0)),
            pl.BlockSpec((None, None, d, tf), lambda i, j: (layer, nf + j, 0, 0)),
            pl.BlockSpec((None, tf, d), lambda i, j: (layer, j, 0)),
        ],
        out_specs=pl.BlockSpec((tm, d), row),
        out_shape=jax.ShapeDtypeStruct((n, d), F32),
        scratch_shapes=[pltpu.VMEM((tm, d), BF16), pltpu.VMEM((tm, d), F32),
                        pltpu.VMEM((tm, 1), F32)],
        compiler_params=pltpu.CompilerParams(
            dimension_semantics=("parallel", "arbitrary"), vmem_limit_bytes=VMEM_LIMIT),
        name="ffn",
    )(x, shift, scale, gate, gain_a, gain_b, w13, w13, w2)


def _inproj_kernel(x_ref, sh_ref, sc_ref, ga_ref, w_ref, o_ref, h_ref, r_ref):
    @pl.when(pl.program_id(1) == 0)
    def _():
        _norm_modulate_tile(x_ref, r_ref, ga_ref, sh_ref, sc_ref, h_ref)

    o_ref[...] = _dot(h_ref[...], w_ref[...])


def _inproj(x, shift, scale, gain, w_in, layer, tiles_per_batch, tm):
    n, d = x.shape
    n_tiles, tn = w_in.shape[1], w_in.shape[3]
    per_batch = lambda i, j: (i // tiles_per_batch, 0, 0)
    return pl.pallas_call(
        _inproj_kernel,
        grid=(n // tm, n_tiles),
        in_specs=[
            pl.BlockSpec((tm, d), lambda i, j: (i, 0)),
            pl.BlockSpec((None, 1, d), per_batch),
            pl.BlockSpec((None, 1, d), per_batch),
            pl.BlockSpec((1, d), lambda i, j: (0, 0)),
            pl.BlockSpec((None, None, d, tn), lambda i, j: (layer, j, 0, 0)),
        ],
        out_specs=pl.BlockSpec((tm, tn), lambda i, j: (i, j)),
        out_shape=jax.ShapeDtypeStruct((n, n_tiles * tn), F32),
        scratch_shapes=[pltpu.VMEM((tm, d), BF16), pltpu.VMEM((tm, 1), F32)],
        compiler_params=pltpu.CompilerParams(
            dimension_semantics=("parallel", "arbitrary"), vmem_limit_bytes=VMEM_LIMIT),
        name="inproj",
    )(x, shift, scale, gain, w_in)


def _dilated_log_counts(tq, tk):
    reach = max(w for w, _ in DILATED_BRANCHES)
    nd = (reach + tq - 1) // tk + 1
    d = np.arange(nd)[:, None, None]
    i = np.arange(tq)[None, :, None]
    j = np.arange(tk)[None, None, :]
    delta = d * tk + i - j
    cnt = np.zeros(delta.shape, np.float64)
    for window, dil in DILATED_BRANCHES:
        cnt += (delta >= 0) & (delta <= window) & (delta % dil == 0)
    table = np.full((nd + 1, tq, tk), NEG_BIG, np.float32)
    table[:nd] = np.where(cnt > 0, np.log(np.maximum(cnt, 1.0)), NEG_BIG)
    return table


def _attn_a_kernel(q_ref, k_ref, v_ref, lc_ref, gain_ref, o_ref, kb_ref, vb_ref, s_ref, *, tq, tk, nd):
    qi = pl.program_id(2)

    @pl.when(qi == 0)
    def _():
        kb_ref[...] = k_ref[...].astype(BF16)
        vb_ref[:, 0:HEAD_DIM] = v_ref[...].astype(BF16)
        vb_ref[:, HEAD_DIM:] = jnp.ones((vb_ref.shape[0], HEAD_DIM), BF16)

    q = q_ref[...].astype(BF16)
    scale = HEAD_DIM ** -0.5
    half = tk // 2

    maxes, accs = [], []
    for g0 in range(0, nd, DILATED_GROUP):
        group = range(g0, min(g0 + DILATED_GROUP, nd))
        m_run = jnp.full((tq, half), NEG_BIG, F32)
        offsets = {}
        for d in group:
            c = qi - d
            offsets[d] = pl.multiple_of(jnp.maximum(c, 0) * tk, tk)
            t = (_dot_nt(q, kb_ref[pl.ds(offsets[d], tk), :]) * scale
                 + lc_ref[jnp.where(c >= 0, d, nd)])
            s_ref[d] = t
            m_run = jnp.maximum(m_run, jnp.maximum(t[:, :half], t[:, half:]))
        m = jnp.max(m_run, axis=-1, keepdims=True)
        acc = jnp.zeros((tq, 2 * HEAD_DIM), F32)
        for d in group:
            p = jnp.exp(s_ref[d] - m)
            acc = acc + _dot(p.astype(BF16), vb_ref[pl.ds(offsets[d], tk), :])
        maxes.append(m)
        accs.append(acc)
    m_all = functools.reduce(jnp.maximum, maxes)
    acc = functools.reduce(lambda a, b: a + b,
                           [jnp.exp(m - m_all) * a for m, a in zip(maxes, accs)])
    o = acc[:, :HEAD_DIM] / acc[:, HEAD_DIM:]
    o_ref[...] = _rms(o, gain_ref[...]).astype(BF16)


def _attn_a(proj, gains, layer, batch, seq, gh, col0, tq, tk):
    table = jnp.asarray(_dilated_log_counts(tq, tk))
    nd = table.shape[0] - 1
    nq = seq // tq
    return pl.pallas_call(
        functools.partial(_attn_a_kernel, tq=tq, tk=tk, nd=nd),
        grid=(batch, gh, nq),
        in_specs=[
            pl.BlockSpec((tq, HEAD_DIM), lambda b, h, i: (b * nq + i, col0 + h)),
            pl.BlockSpec((seq, HEAD_DIM), lambda b, h, i: (b, col0 + gh + h)),
            pl.BlockSpec((seq, HEAD_DIM), lambda b, h, i: (b, col0 + 2 * gh + h)),
            pl.BlockSpec((nd + 1, tq, tk), lambda b, h, i: (0, 0, 0)),
            pl.BlockSpec((None, None, 1, HEAD_DIM), lambda b, h, i: (layer, h, 0, 0)),
        ],
        out_specs=pl.BlockSpec((tq, HEAD_DIM), lambda b, h, i: (b * nq + i, h)),
        out_shape=jax.ShapeDtypeStruct((batch * seq, gh * HEAD_DIM), BF16),
        scratch_shapes=[pltpu.VMEM((seq, HEAD_DIM), BF16), pltpu.VMEM((seq, 2 * HEAD_DIM), BF16),
                        pltpu.VMEM((nd, tq, tk), F32)],
        compiler_params=pltpu.CompilerParams(
            dimension_semantics=("parallel", "parallel", "arbitrary"),
            vmem_limit_bytes=VMEM_LIMIT),
        name="dilated_attention",
    )(proj, proj, proj, table, gains)


STICK_UNDERFLOW = 100.0
STICK_BOUND_SLACK = 1.01


def _attn_b_kernel(q_ref, k_ref, v_ref, gain_ref, o_ref, kb_ref, vb_ref, kmax_ref, *, tq):
    qi = pl.program_id(2)
    tk = tq

    @pl.when(qi == 0)
    def _():
        k = k_ref[...]
        kb_ref[...] = k.astype(BF16)
        vb_ref[...] = v_ref[...].astype(BF16)
        k_sq = jnp.max(jnp.sum(k * k, axis=-1, keepdims=True), axis=0, keepdims=True)
        kmax_ref[...] = jnp.broadcast_to(jnp.sqrt(k_sq), kmax_ref.shape)

    qf = q_ref[...]
    q = qf.astype(BF16)
    scale = HEAD_DIM ** -0.5
    q_norm = jnp.sqrt(jnp.sum(qf * qf, axis=-1, keepdims=True))
    z_bound = (scale * STICK_BOUND_SLACK) * q_norm * kmax_ref[0:1, 0:1]
    jj = lax.broadcasted_iota(jnp.int32, (tk, tk), 0)
    ss = lax.broadcasted_iota(jnp.int32, (tk, tk), 1)
    suffix = jnp.where(jj >= ss, 1.0, 0.0).astype(BF16)

    def logits(c, keep):
        off = pl.multiple_of(c * tk, tk)
        z = _dot_nt(q, kb_ref[pl.ds(off, tk), :]) * scale
        log_keep = _log_sigmoid(-z)
        if keep is not None:
            log_keep = jnp.where(keep, log_keep, 0.0)
        return z, log_keep, vb_ref[pl.ds(off, tk), :]

    def suffix_sums(log_keep):
        lk_hi = log_keep.astype(BF16)
        lk_lo = (log_keep - lk_hi.astype(F32)).astype(BF16)
        return _dot(lk_hi, suffix) + _dot(lk_lo, suffix)

    def live(tail):
        return (jnp.max(tail + z_bound) > -STICK_UNDERFLOW).astype(jnp.int32)

    causal = ss < jj
    has_left = qi > 0
    z_d, lk_d, v_d = logits(qi, causal)
    z_l, lk_l, v_l = logits(jnp.maximum(qi - 1, 0), has_left)
    within_d = suffix_sums(lk_d)
    within_l = suffix_sums(lk_l) + within_d[:, 0:1]
    a_d = jnp.where(causal, jnp.exp(z_d + within_d), 0.0)
    a_l = jnp.where(has_left, jnp.exp(z_l + within_l), 0.0)
    acc = _dot(a_d.astype(BF16), v_d) + _dot(a_l.astype(BF16), v_l)
    tail = within_l[:, 0:1]

    def cond(carry):
        c, go, _, _ = carry
        return jnp.logical_and(c >= 0, go > 0)

    def body(carry):
        c, _, tail, acc = carry
        z, log_keep, v = logits(c, None)
        within = suffix_sums(log_keep) + tail
        acc = acc + _dot(jnp.exp(z + within).astype(BF16), v)
        tail = within[:, 0:1]
        return c - 1, live(tail), tail, acc

    _, _, _, acc = lax.while_loop(cond, body, (qi - 2, live(tail), tail, acc))
    o_ref[...] = _rms(acc, gain_ref[...]).astype(BF16)


def _attn_b(proj, gains, layer, batch, seq, gh, col0, tq):
    nq = seq // tq
    return pl.pallas_call(
        functools.partial(_attn_b_kernel, tq=tq),
        grid=(batch, gh, nq),
        in_specs=[
            pl.BlockSpec((tq, HEAD_DIM), lambda b, h, i: (b * nq + i, col0 + h)),
            pl.BlockSpec((seq, HEAD_DIM), lambda b, h, i: (b, col0 + gh + h)),
            pl.BlockSpec((seq, HEAD_DIM), lambda b, h, i: (b, col0 + 2 * gh + h)),
            pl.BlockSpec((None, None, 1, HEAD_DIM), lambda b, h, i: (layer, gh + h, 0, 0)),
        ],
        out_specs=pl.BlockSpec((tq, HEAD_DIM), lambda b, h, i: (b * nq + i, h)),
        out_shape=jax.ShapeDtypeStruct((batch * seq, gh * HEAD_DIM), BF16),
        scratch_shapes=[pltpu.VMEM((seq, HEAD_DIM), BF16), pltpu.VMEM((seq, HEAD_DIM), BF16),
                        pltpu.VMEM((8, HEAD_DIM), F32)],
        compiler_params=pltpu.CompilerParams(
            dimension_semantics=("parallel", "parallel", "arbitrary"),
            vmem_limit_bytes=VMEM_LIMIT),
        name="stick_breaking_attention",
    )(proj, proj, proj, gains)


def _hgrn_chunk(q, k, v, b, state_t):
    c = HGRN_CHUNK
    nsub = c // HGRN_SUB
    o_inter = _dot_nt((q * jnp.exp(b)).astype(BF16), state_t.astype(BF16))
    row_id = lax.broadcasted_iota(jnp.int32, (HGRN_SUB, 1), 0)
    blocks = [o_inter[HGRN_SUB * i:HGRN_SUB * (i + 1)] for i in range(nsub)]
    for sub in range(nsub):
        lo = sub * HGRN_SUB
        hi = lo + HGRN_SUB
        q_d, b_d = q[lo:hi], b[lo:hi]
        q_r, b_r = q[hi:], b[hi:]
        diag = None
        rest = None
        for si in range(HGRN_SUB):
            s = lo + si
            b_s, k_s, v_s = b[s:s + 1], k[s:s + 1], v[s:s + 1]
            decay = jnp.exp(jnp.minimum(b_d - b_s, 0.0))
            score = jnp.sum(q_d * decay * k_s, axis=-1, keepdims=True)
            if si > 0:
                score = jnp.where(row_id >= si, score, 0.0)
            term = score * v_s
            diag = term if diag is None else diag + term
            if sub < nsub - 1:
                score = jnp.sum(q_r * jnp.exp(b_r - b_s) * k_s, axis=-1, keepdims=True)
                term = score * v_s
                rest = term if rest is None else rest + term
        blocks[sub] = blocks[sub] + diag
        for i in range(sub + 1, nsub):
            r = (i - sub - 1) * HGRN_SUB
            blocks[i] = blocks[i] + rest[r:r + HGRN_SUB]
    o = jnp.concatenate(blocks, axis=0)
    b_last = b[c - 1:c]
    k_decayed = k * jnp.exp(b_last - b)
    new_state_t = jnp.exp(b_last) * state_t + _dot_tn(v.astype(BF16), k_decayed.astype(BF16))
    return o, new_state_t


def _hgrn_kernel(cq_ref, cf_ref, ci_ref, cg_ref, lb_ref, gain_ref, o_ref, state_ref, of_ref, *, gh):
    ts = cq_ref.shape[0]
    c = HGRN_CHUNK

    @pl.when(pl.program_id(1) == 0)
    def _():
        state_ref[...] = jnp.zeros_like(state_ref)

    lb = lb_ref[...]
    log_lb = jnp.log(jnp.maximum(lb, LB_FLOOR))
    log_1m_lb = jnp.log1p(-lb)
    ii = lax.broadcasted_iota(jnp.int32, (c, c), 0)
    jj = lax.broadcasted_iota(jnp.int32, (c, c), 1)
    prefix = jnp.where(jj <= ii, 1.0, 0.0).astype(BF16)

    def chunk(ci, _):
        base = pl.multiple_of(ci * c, c)
        rows = pl.ds(base, c)
        cf = cf_ref[rows, :]
        x = log_1m_lb + _log_sigmoid(cf)
        hi = jnp.maximum(log_lb, x)
        log_f = hi + jnp.log1p(jnp.exp(-jnp.abs(log_lb - x)))
        b_all = _dot_exact_lhs(prefix, log_f)
        k_all = (1.0 - lb) * _sigmoid(-cf)
        q_all = _silu(cq_ref[rows, :])
        v_all = ci_ref[rows, :]
        for h in range(gh):
            cols = slice(h * HEAD_DIM, (h + 1) * HEAD_DIM)
            o, st = _hgrn_chunk(q_all[:, cols], k_all[:, cols], v_all[:, cols], b_all[:, cols],
                                state_ref[h])
            state_ref[h] = st
            of_ref[rows, cols] = o
        return 0

    lax.fori_loop(0, ts // c, chunk, 0, unroll=2)

    for h in range(gh):
        cols = slice(h * HEAD_DIM, (h + 1) * HEAD_DIM)
        o = _rms(of_ref[:, cols], gain_ref[:, cols]) * _silu(cg_ref[:, cols])
        o_ref[:, cols] = o.astype(BF16)


def _hgrn(proj, lower_bound, gains, layer, batch, seq, gh, col0, ts):
    gw = gh * HEAD_DIM
    nt = seq // ts
    blk = lambda k: pl.BlockSpec((ts, gw), lambda b, i, k=k: (b * nt + i, col0 + k))
    return pl.pallas_call(
        functools.partial(_hgrn_kernel, gh=gh),
        grid=(batch, nt),
        in_specs=[
            blk(0), blk(1), blk(2), blk(3),
            pl.BlockSpec((None, 1, gw), lambda b, i: (layer, 0, 0)),
            pl.BlockSpec((None, None, 1, gw), lambda b, i: (layer, 2, 0, 0)),
        ],
        out_specs=pl.BlockSpec((ts, gw), lambda b, i: (b * nt + i, 0)),
        out_shape=jax.ShapeDtypeStruct((batch * seq, gw), BF16),
        scratch_shapes=[pltpu.VMEM((gh, HEAD_DIM, HEAD_DIM), F32), pltpu.VMEM((ts, gw), F32)],
        compiler_params=pltpu.CompilerParams(
            dimension_semantics=("parallel", "arbitrary"), vmem_limit_bytes=VMEM_LIMIT),
        name="hgrn2",
    )(proj, proj, proj, proj, lower_bound, gains)


def _gdn_kernel(xq_ref, xk_ref, xv_ref, z_ref, ab_ref, wq_ref, wk_ref, wv_ref, alog_ref, dt_ref,
                gain_ref, o_ref, pad_ref, act_ref, state_ref, of_ref, *, gh):
    ts = xq_ref.shape[0]
    c = GDN_CHUNK
    first = pl.program_id(1) == 0

    @pl.when(first)
    def _():
        state_ref[...] = jnp.zeros_like(state_ref)
        pad_ref[:, 0:CONV_HALO, :] = jnp.zeros((3, CONV_HALO, pad_ref.shape[2]), F32)

    @pl.when(jnp.logical_not(first))
    def _():
        pad_ref[:, 0:CONV_HALO, :] = pad_ref[:, ts:ts + CONV_HALO, :]

    for part, (x_ref, w_ref) in enumerate(((xq_ref, wq_ref), (xk_ref, wk_ref), (xv_ref, wv_ref))):
        pad_ref[part, CONV_HALO:CONV_HALO + ts, :] = x_ref[...]
        acc = None
        for tap in range(CONV_WIDTH):
            start = CONV_HALO - (CONV_WIDTH - 1) + tap
            term = w_ref[tap:tap + 1, :] * pad_ref[part, start:start + ts, :]
            acc = term if acc is None else acc + term
        act_ref[part] = _silu(acc)

    ab = ab_ref[:, 0:HEAD_DIM]
    log_alpha = -jnp.exp(alog_ref[...]) * _softplus(ab + dt_ref[...])
    beta_all = _sigmoid(ab)

    ii = lax.broadcasted_iota(jnp.int32, (c, c), 0)
    jj = lax.broadcasted_iota(jnp.int32, (c, c), 1)
    incl = jj <= ii
    strict = jj < ii
    prefix = jnp.where(incl, 1.0, 0.0).astype(BF16)
    base_shift = int(np.log2(GDN_INV_BASE))
    in_base = jnp.where((ii >> base_shift) == (jj >> base_shift), 1.0, 0.0)
    merge_masks = []
    for shift in range(base_shift, int(np.log2(c))):
        same_pair = (ii >> (shift + 1)) == (jj >> (shift + 1))
        other_half = (ii >> shift) != (jj >> shift)
        merge_masks.append(jnp.where(same_pair & other_half, 1.0, 0.0))

    n_chunks = ts // c
    bodies = [(ci, h) for ci in range(n_chunks) for h in range(gh)]
    g_all, g_rows = [], []
    for ci in range(n_chunks):
        g = _dot_exact_lhs(prefix, log_alpha[ci * c:(ci + 1) * c])
        g_all.append(g)
        g_rows.append(g.T)

    def load(part, ci, h):
        return act_ref[part, ci * c:(ci + 1) * c, h * HEAD_DIM:(h + 1) * HEAD_DIM]

    def l2norm(t):
        return t * lax.rsqrt(jnp.sum(t * t, axis=-1, keepdims=True) + EPS)

    q = [l2norm(load(0, ci, h)) * (HEAD_DIM ** -0.5) for ci, h in bodies]
    k = [l2norm(load(1, ci, h)) for ci, h in bodies]
    v = [load(2, ci, h) for ci, h in bodies]
    beta = [beta_all[ci * c:(ci + 1) * c, gh + h:gh + h + 1] for ci, h in bodies]
    g_col = [g_all[ci][:, h:h + 1] for ci, h in bodies]
    g_last = [g_all[ci][c - 1:c, h:h + 1] for ci, h in bodies]
    decay = [jnp.where(incl, jnp.exp(jnp.minimum(gc - g_rows[ci][h:h + 1, :], 0.0)), 0.0)
             for gc, (ci, h) in zip(g_col, bodies)]
    k_beta = [ki * bi for ki, bi in zip(k, beta)]
    k16 = [ki.astype(BF16) for ki in k]
    a = [jnp.where(strict, _dot_nt(kb.astype(BF16), kh) * dc, 0.0)
         for kb, kh, dc in zip(k_beta, k16, decay)]
    power = [ai * in_base for ai in a]
    n = [-p for p in power]
    for _ in range(base_shift - 1):
        power = [_dot16(p, p) for p in power]
        n = [ni + p + _dot16(ni, p) for ni, p in zip(n, power)]
    for mask in merge_masks:
        off = [ai * mask for ai in a]
        x = [o + _dot16(ni, o) for ni, o in zip(n, off)]
        n = [ni - (xi + _dot16(xi, ni)) for ni, xi in zip(n, x)]
    rhs = [jnp.concatenate([vi * bi, kb * jnp.exp(gc)], axis=1)
           for vi, bi, kb, gc in zip(v, beta, k_beta, g_col)]
    uw = [r + _dot16(ni, r) for ni, r in zip(n, rhs)]
    qk = [(_dot_nt(qi.astype(BF16), kh) * dc).astype(BF16) for qi, kh, dc in zip(q, k16, decay)]
    q_in = [(qi * jnp.exp(gc)).astype(BF16) for qi, gc in zip(q, g_col)]
    k_out = [(ki * jnp.exp(gl - gc)).astype(BF16) for ki, gl, gc in zip(k, g_last, g_col)]

    for i, (ci, h) in enumerate(bodies):
        state = state_ref[h]
        s16 = state.astype(BF16)
        v_new = uw[i][:, :HEAD_DIM] - _dot(uw[i][:, HEAD_DIM:].astype(BF16), s16)
        vn16 = v_new.astype(BF16)
        o = _dot(q_in[i], s16) + _dot(qk[i], vn16)
        state_ref[h] = jnp.exp(g_last[i]) * state + _dot_tn(k_out[i], vn16)
        of_ref[ci * c:(ci + 1) * c, h * HEAD_DIM:(h + 1) * HEAD_DIM] = o

    for h in range(gh):
        cols = slice(h * HEAD_DIM, (h + 1) * HEAD_DIM)
        o = _rms(of_ref[:, cols], gain_ref[:, cols]) * _silu(z_ref[:, cols])
        o_ref[:, cols] = o.astype(BF16)


def _gdn(proj, conv_w, alog_row, dt_row, gains, layer, batch, seq, gh, col0, ts):
    gw = gh * HEAD_DIM
    nt = seq // ts
    blk = lambda k: pl.BlockSpec((ts, gw), lambda b, i, k=k: (b * nt + i, col0 + k))
    wblk = lambda k: pl.BlockSpec((None, CONV_WIDTH, gw), lambda b, i, k=k: (layer, 0, k))
    row = pl.BlockSpec((None, 1, HEAD_DIM), lambda b, i: (layer, 0, 0))
    return pl.pallas_call(
        functools.partial(_gdn_kernel, gh=gh),
        grid=(batch, nt),
        in_specs=[
            blk(0), blk(1), blk(2), blk(3), blk(4),
            wblk(0), wblk(1), wblk(2), row, row,
            pl.BlockSpec((None, None, 1, gw), lambda b, i: (layer, 3, 0, 0)),
        ],
        out_specs=pl.BlockSpec((ts, gw), lambda b, i: (b * nt + i, 0)),
        out_shape=jax.ShapeDtypeStruct((batch * seq, gw), BF16),
        scratch_shapes=[
            pltpu.VMEM((3, ts + CONV_HALO, gw), F32),
            pltpu.VMEM((3, ts, gw), F32),
            pltpu.VMEM((gh, HEAD_DIM, HEAD_DIM), F32),
            pltpu.VMEM((ts, gw), F32),
        ],
        compiler_params=pltpu.CompilerParams(
            dimension_semantics=("parallel", "arbitrary"), vmem_limit_bytes=VMEM_LIMIT),
        name="gated_deltanet",
    )(proj, proj, proj, proj, proj, conv_w, conv_w, conv_w, alog_row, dt_row, gains)


def _outproj_kernel(x_ref, oa_ref, ob_ref, oc_ref, od_ref, w_ref, gt_ref, gain_ref, o_ref):
    gw = oa_ref.shape[1]
    y = None
    for g, m_ref in enumerate((oa_ref, ob_ref, oc_ref, od_ref)):
        part = _dot(m_ref[...], w_ref[g * gw:(g + 1) * gw, :])
        y = part if y is None else y + part
    o_ref[...] = x_ref[...] + gt_ref[...] * _rms(y, gain_ref[...])


def _outproj(x, groups, w_out, gate, gain, layer, tiles_per_batch, tm):
    n, d = x.shape
    gw = groups[0].shape[1]
    gspec = pl.BlockSpec((tm, gw), lambda i: (i, 0))
    return pl.pallas_call(
        _outproj_kernel,
        grid=(n // tm,),
        in_specs=[
            pl.BlockSpec((tm, d), lambda i: (i, 0)),
            gspec, gspec, gspec, gspec,
            pl.BlockSpec((None, N_GROUPS * gw, d), lambda i: (layer, 0, 0)),
            pl.BlockSpec((None, 1, d), lambda i: (i // tiles_per_batch, 0, 0)),
            pl.BlockSpec((1, d), lambda i: (0, 0)),
        ],
        out_specs=pl.BlockSpec((tm, d), lambda i: (i, 0)),
        out_shape=jax.ShapeDtypeStruct((n, d), F32),
        compiler_params=pltpu.CompilerParams(
            dimension_semantics=("parallel",), vmem_limit_bytes=VMEM_LIMIT),
        name="outproj",
    )(x, *groups, w_out, gate, gain)


def _largest_tile(n, cap):
    t = min(n, cap)
    while n % t:
        t //= 2
    return t


def kernel(x, c, w_mod, b_mod, norm_gain, w_in, w_out, mix_out_gain, hgrn_lb_logits, dn_conv_w,
           dn_a_log, dn_dt_bias, ffn1_w13, ffn1_w2, ffn2_w13, ffn2_w2):
    batch, seq, d = x.shape
    depth = w_mod.shape[0]
    gw = d // N_GROUPS
    gh = gw // HEAD_DIM
    dff = ffn1_w2.shape[1]
    n = batch * seq

    tm_ffn = _largest_tile(seq, 512)
    tf = dff // 11 if dff % 11 == 0 and (dff // 11) % 128 == 0 else _largest_tile(dff, 512)
    tm_proj = _largest_tile(seq, 1024)
    tm_out = _largest_tile(seq, 512)
    tq = _largest_tile(seq, 256)
    ts_c = _largest_tile(seq, 1024)
    ts_d = _largest_tile(seq, 512)

    in_cols = w_in.shape[2]
    pad_cols = (-in_cols) % gw
    w_in16 = _column_tiles(jnp.pad(w_in, ((0, 0), (0, 0), (0, pad_cols))).astype(BF16), gw)
    w_out16 = w_out.astype(BF16)
    ffn_w = [(_column_tiles(ffn1_w13.astype(BF16), tf), ffn1_w2.astype(BF16)),
             (_column_tiles(ffn2_w13.astype(BF16), tf), ffn2_w2.astype(BF16))]
    lb_p = jax.nn.softmax(hgrn_lb_logits.astype(F32), axis=0)
    lower_bounds = (jnp.cumsum(lb_p, axis=0) - lb_p[0]).reshape(depth, 1, gw)
    gains = mix_out_gain.reshape(depth, N_GROUPS, 1, gw)
    head_gains = mix_out_gain.reshape(depth, N_GROUPS * gh, 1, HEAD_DIM)
    lane_pad = ((0, 0), (0, HEAD_DIM - gh))
    alog_row = jnp.pad(dn_a_log.astype(F32), lane_pad).reshape(depth, 1, HEAD_DIM)
    dt_row = jnp.pad(dn_dt_bias.astype(F32), lane_pad).reshape(depth, 1, HEAD_DIM)
    conv_w = dn_conv_w.astype(F32)

    c_rows = -(-batch // 8) * 8
    c_pad = jnp.pad(c, ((0, c_rows - batch), (0, 0)))
    mod = _modulation(c_pad, w_mod, b_mod, _largest_tile(N_MOD * d, 2048))[:, :batch]
    mod = mod.reshape(depth, batch, N_MOD, 1, d)

    xf = x.reshape(n, d)
    for l in range(depth):
        sh1, sc1, g1, sh2, sc2, g2, sh3, sc3, g3 = (mod[l, :, i] for i in range(N_MOD))
        ng = norm_gain[l].reshape(6, 1, d)
        xf = _ffn(xf, sh1, sc1, g1, ng[0], ng[1], *ffn_w[0], l, seq // tm_ffn, tm_ffn)
        proj = _inproj(xf, sh2, sc2, ng[2], w_in16, l, seq // tm_proj, tm_proj)
        hb = gw // HEAD_DIM
        o_a = _attn_a(proj, head_gains, l, batch, seq, gh, 0 * hb, tq, tq)
        o_b = _attn_b(proj, head_gains, l, batch, seq, gh, 3 * hb, tq)
        o_c = _hgrn(proj, lower_bounds, gains, l, batch, seq, gh, 6, ts_c)
        o_d = _gdn(proj, conv_w, alog_row, dt_row, gains, l, batch, seq, gh, 10, ts_d)
        xf = _outproj(xf, (o_a, o_b, o_c, o_d), w_out16, g2, ng[3], l, seq // tm_out, tm_out)
        xf = _ffn(xf, sh3, sc3, g3, ng[4], ng[5], *ffn_w[1], l, seq // tm_ffn, tm_ffn)
    return xf.reshape(batch, seq, d)
```

```python
import functools

import jax
import jax.numpy as jnp
import numpy as np
from jax import lax
from jax.experimental import pallas as pl
from jax.experimental.pallas import tpu as pltpu

F32 = jnp.float32
BF16 = jnp.bfloat16

HEAD_DIM = 128
N_GROUPS = 4
N_MOD = 9
CONV_WIDTH = 4
DILATED_BRANCHES = ((128, 1), (512, 4), (2048, 16))
DILATED_GROUP = 2
EPS = 1e-6
NEG_BIG = -1e30
LB_FLOOR = 1e-30

HGRN_CHUNK = 32
HGRN_SUB = 8
GDN_CHUNK = 128
GDN_INV_BASE = 8
CONV_HALO = 8
NORM_ROWS = 128
VMEM_LIMIT = 56 * 1024 * 1024

_NT = (((1,), (1,)), ((), ()))
_TN = (((0,), (0,)), ((), ()))


def _dot(a, b):
    return jnp.dot(a, b, preferred_element_type=F32)


def _dot_nt(a, b):
    return lax.dot_general(a, b, _NT, preferred_element_type=F32)


def _dot_tn(a, b):
    return lax.dot_general(a, b, _TN, preferred_element_type=F32)


def _split3(a):
    hi = a.astype(BF16)
    r = a - hi.astype(F32)
    mid = r.astype(BF16)
    lo = (r - mid.astype(F32)).astype(BF16)
    return hi, mid, lo


def _dot_exact_lhs(a_bf16, b):
    hi, mid, lo = _split3(b)
    return _dot(a_bf16, hi) + (_dot(a_bf16, mid) + _dot(a_bf16, lo))


def _dot16(a, b):
    return _dot(a.astype(BF16), b.astype(BF16))


def _sigmoid(x):
    return 1.0 / (1.0 + jnp.exp(-x))


def _silu(x):
    return x * _sigmoid(x)


def _softplus(x):
    return jnp.maximum(x, 0.0) + jnp.log1p(jnp.exp(-jnp.abs(x)))


def _log_sigmoid(x):
    return jnp.minimum(x, 0.0) - jnp.log1p(jnp.exp(-jnp.abs(x)))


def _rms(x, gain):
    return x * lax.rsqrt(jnp.mean(x * x, axis=-1, keepdims=True) + EPS) * gain


def _mod_kernel(c_ref, w_ref, b_ref, o_ref):
    cond = _silu(c_ref[...]).astype(BF16)
    o_ref[...] = _dot(cond, w_ref[...].astype(BF16)) + b_ref[...]


def _modulation(c_pad, w_mod, b_mod, tn):
    depth, d, n = w_mod.shape
    rows = c_pad.shape[0]
    return pl.pallas_call(
        _mod_kernel,
        grid=(depth, n // tn),
        in_specs=[
            pl.BlockSpec((rows, d), lambda l, j: (0, 0)),
            pl.BlockSpec((None, d, tn), lambda l, j: (l, 0, j)),
            pl.BlockSpec((None, 1, tn), lambda l, j: (l, 0, j)),
        ],
        out_specs=pl.BlockSpec((None, rows, tn), lambda l, j: (l, 0, j)),
        out_shape=jax.ShapeDtypeStruct((depth, rows, n), F32),
        compiler_params=pltpu.CompilerParams(
            dimension_semantics=("parallel", "parallel"), vmem_limit_bytes=VMEM_LIMIT),
        name="modulation",
    )(c_pad, w_mod, b_mod.reshape(depth, 1, n))


def _norm_modulate(x, gain, shift, scale):
    return (_rms(x, gain) * (1.0 + scale) + shift).astype(BF16)


def _for_row_slices(n_rows, body):
    def step(r, carry):
        body(pl.ds(pl.multiple_of(r * NORM_ROWS, NORM_ROWS), NORM_ROWS))
        return carry

    lax.fori_loop(0, n_rows // NORM_ROWS, step, 0)


def _inv_rms(x):
    return lax.rsqrt(jnp.mean(x * x, axis=-1, keepdims=True) + EPS)


def _norm_modulate_tile(x_ref, r_ref, ga_ref, sh_ref, sc_ref, h_ref, also=None):
    r_ref[...] = _inv_rms(x_ref[...])

    def body(rows):
        y = x_ref[rows, :] * r_ref[rows, :] * ga_ref[...]
        h_ref[rows, :] = (y * (1.0 + sc_ref[...]) + sh_ref[...]).astype(BF16)
        if also is not None:
            also(rows)

    _for_row_slices(x_ref.shape[0], body)


def _ffn_kernel(x_ref, sh_ref, sc_ref, gt_ref, ga_ref, gb_ref, w1_ref, w3_ref, w2_ref,
                o_ref, h_ref, acc_ref, r_ref):
    j = pl.program_id(1)

    @pl.when(j == 0)
    def _():
        def clear(rows):
            acc_ref[rows, :] = jnp.zeros((NORM_ROWS, acc_ref.shape[1]), F32)

        _norm_modulate_tile(x_ref, r_ref, ga_ref, sh_ref, sc_ref, h_ref, also=clear)

    h = h_ref[...]
    gate = _dot(h, w1_ref[...])
    up = _dot(h, w3_ref[...])
    act = (_silu(gate) * up).astype(BF16)
    acc_ref[...] += _dot(act, w2_ref[...])

    @pl.when(j == pl.num_programs(1) - 1)
    def _():
        r_ref[...] = _inv_rms(acc_ref[...])

        def body(rows):
            y = acc_ref[rows, :] * r_ref[rows, :] * gb_ref[...]
            o_ref[rows, :] = x_ref[rows, :] + (0.5 * gt_ref[...]) * y

        _for_row_slices(x_ref.shape[0], body)


def _ffn(x, shift, scale, gate, gain_a, gain_b, w13, w2, layer, tiles_per_batch, tm, tf):
    n, d = x.shape
    dff = w2.shape[1]
    nf = dff // tf
    row = lambda i, j: (i, 0)
    per_batch = lambda i, j: (i // tiles_per_batch, 0, 0)
    const = lambda i, j: (0, 0)
    return pl.pallas_call(
        _ffn_kernel,
        grid=(n // tm, nf),
        in_specs=[
            pl.BlockSpec((tm, d), row),
            pl.BlockSpec((None, 1, d), per_batch),
            pl.BlockSpec((None, 1, d), per_batch),
            pl.BlockSpec((None, 1, d), per_batch),
            pl.BlockSpec((1, d), const),
            pl.BlockSpec((1, d), const),
            pl.BlockSpec((None, d, tf), lambda i, j: (layer, 0, j)),
            pl.BlockSpec((None, d, tf), lambda i, j: (layer, 0, nf + j)),
            pl.BlockSpec((None, tf, d), lambda i, j: (layer, j, 0)),
        ],
        out_specs=pl.BlockSpec((tm, d), row),
        out_shape=jax.ShapeDtypeStruct((n, d), F32),
        scratch_shapes=[pltpu.VMEM((tm, d), BF16), pltpu.VMEM((tm, d), F32),
                        pltpu.VMEM((tm, 1), F32)],
        compiler_params=pltpu.CompilerParams(
            dimension_semantics=("parallel", "arbitrary"), vmem_limit_bytes=VMEM_LIMIT),
        name="ffn",
    )(x, shift, scale, gate, gain_a, gain_b, w13, w13, w2)


def _inproj_kernel(x_ref, sh_ref, sc_ref, ga_ref, w_ref, o_ref, h_ref, r_ref):
    @pl.when(pl.program_id(1) == 0)
    def _():
        _norm_modulate_tile(x_ref, r_ref, ga_ref, sh_ref, sc_ref, h_ref)

    o_ref[...] = _dot(h_ref[...], w_ref[...])


def _inproj(x, shift, scale, gain, w_in, layer, tiles_per_batch, tm, tn):
    n, d = x.shape
    cols = w_in.shape[2]
    per_batch = lambda i, j: (i // tiles_per_batch, 0, 0)
    return pl.pallas_call(
        _inproj_kernel,
        grid=(n // tm, cols // tn),
        in_specs=[
            pl.BlockSpec((tm, d), lambda i, j: (i, 0)),
            pl.BlockSpec((None, 1, d), per_batch),
            pl.BlockSpec((None, 1, d), per_batch),
            pl.BlockSpec((1, d), lambda i, j: (0, 0)),
            pl.BlockSpec((None, d, tn), lambda i, j: (layer, 0, j)),
        ],
        out_specs=pl.BlockSpec((tm, tn), lambda i, j: (i, j)),
        out_shape=jax.ShapeDtypeStruct((n, cols), F32),
        scratch_shapes=[pltpu.VMEM((tm, d), BF16), pltpu.VMEM((tm, 1), F32)],
        compiler_params=pltpu.CompilerParams(
            dimension_semantics=("parallel", "arbitrary"), vmem_limit_bytes=VMEM_LIMIT),
        name="inproj",
    )(x, shift, scale, gain, w_in)


def _dilated_log_counts(tq, tk):
    reach = max(w for w, _ in DILATED_BRANCHES)
    nd = (reach + tq - 1) // tk + 1
    d = np.arange(nd)[:, None, None]
    i = np.arange(tq)[None, :, None]
    j = np.arange(tk)[None, None, :]
    delta = d * tk + i - j
    cnt = np.zeros(delta.shape, np.float64)
    for window, dil in DILATED_BRANCHES:
        cnt += (delta >= 0) & (delta <= window) & (delta % dil == 0)
    table = np.full((nd + 1, tq, tk), NEG_BIG, np.float32)
    table[:nd] = np.where(cnt > 0, np.log(np.maximum(cnt, 1.0)), NEG_BIG)
    return table


def _attn_a_kernel(q_ref, k_ref, v_ref, lc_ref, gain_ref, o_ref, kb_ref, vb_ref, s_ref, *, tq, tk, nd):
    qi = pl.program_id(2)

    @pl.when(qi == 0)
    def _():
        kb_ref[...] = k_ref[...].astype(BF16)
        vb_ref[:, 0:HEAD_DIM] = v_ref[...].astype(BF16)
        vb_ref[:, HEAD_DIM:] = jnp.ones((vb_ref.shape[0], HEAD_DIM), BF16)

    q = q_ref[...].astype(BF16)
    scale = HEAD_DIM ** -0.5
    half = tk // 2

    maxes, accs = [], []
    for g0 in range(0, nd, DILATED_GROUP):
        group = range(g0, min(g0 + DILATED_GROUP, nd))
        m_run = jnp.full((tq, half), NEG_BIG, F32)
        offsets = {}
        for d in group:
            c = qi - d
            offsets[d] = pl.multiple_of(jnp.maximum(c, 0) * tk, tk)
            t = (_dot_nt(q, kb_ref[pl.ds(offsets[d], tk), :]) * scale
                 + lc_ref[jnp.where(c >= 0, d, nd)])
            s_ref[d] = t
            m_run = jnp.maximum(m_run, jnp.maximum(t[:, :half], t[:, half:]))
        m = jnp.max(m_run, axis=-1, keepdims=True)
        acc = jnp.zeros((tq, 2 * HEAD_DIM), F32)
        for d in group:
            p = jnp.exp(s_ref[d] - m)
            acc = acc + _dot(p.astype(BF16), vb_ref[pl.ds(offsets[d], tk), :])
        maxes.append(m)
        accs.append(acc)
    m_all = functools.reduce(jnp.maximum, maxes)
    acc = functools.reduce(lambda a, b: a + b,
                           [jnp.exp(m - m_all) * a for m, a in zip(maxes, accs)])
    o = acc[:, :HEAD_DIM] / acc[:, HEAD_DIM:]
    o_ref[...] = _rms(o, gain_ref[...]).astype(BF16)


def _attn_a(proj, gains, layer, batch, seq, gh, col0, tq, tk):
    table = jnp.asarray(_dilated_log_counts(tq, tk))
    nd = table.shape[0] - 1
    nq = seq // tq
    return pl.pallas_call(
        functools.partial(_attn_a_kernel, tq=tq, tk=tk, nd=nd),
        grid=(batch, gh, nq),
        in_specs=[
            pl.BlockSpec((tq, HEAD_DIM), lambda b, h, i: (b * nq + i, col0 + h)),
            pl.BlockSpec((seq, HEAD_DIM), lambda b, h, i: (b, col0 + gh + h)),
            pl.BlockSpec((seq, HEAD_DIM), lambda b, h, i: (b, col0 + 2 * gh + h)),
            pl.BlockSpec((nd + 1, tq, tk), lambda b, h, i: (0, 0, 0)),
            pl.BlockSpec((None, None, 1, HEAD_DIM), lambda b, h, i: (layer, h, 0, 0)),
        ],
        out_specs=pl.BlockSpec((tq, HEAD_DIM), lambda b, h, i: (b * nq + i, h)),
        out_shape=jax.ShapeDtypeStruct((batch * seq, gh * HEAD_DIM), BF16),
        scratch_shapes=[pltpu.VMEM((seq, HEAD_DIM), BF16), pltpu.VMEM((seq, 2 * HEAD_DIM), BF16),
                        pltpu.VMEM((nd, tq, tk), F32)],
        compiler_params=pltpu.CompilerParams(
            dimension_semantics=("parallel", "parallel", "arbitrary"),
            vmem_limit_bytes=VMEM_LIMIT),
        name="dilated_attention",
    )(proj, proj, proj, table, gains)


STICK_UNDERFLOW = 100.0
STICK_BOUND_SLACK = 1.01


def _attn_b_kernel(q_ref, k_ref, v_ref, gain_ref, o_ref, kb_ref, vb_ref, kmax_ref, *, tq):
    qi = pl.program_id(2)
    tk = tq

    @pl.when(qi == 0)
    def _():
        k = k_ref[...]
        kb_ref[...] = k.astype(BF16)
        vb_ref[...] = v_ref[...].astype(BF16)
        k_sq = jnp.max(jnp.sum(k * k, axis=-1, keepdims=True), axis=0, keepdims=True)
        kmax_ref[...] = jnp.broadcast_to(jnp.sqrt(k_sq), kmax_ref.shape)

    qf = q_ref[...]
    q = qf.astype(BF16)
    scale = HEAD_DIM ** -0.5
    q_norm = jnp.sqrt(jnp.sum(qf * qf, axis=-1, keepdims=True))
    z_bound = (scale * STICK_BOUND_SLACK) * q_norm * kmax_ref[0:1, 0:1]
    jj = lax.broadcasted_iota(jnp.int32, (tk, tk), 0)
    ss = lax.broadcasted_iota(jnp.int32, (tk, tk), 1)
    suffix = jnp.where(jj >= ss, 1.0, 0.0).astype(BF16)

    def logits(c, keep):
        off = pl.multiple_of(c * tk, tk)
        z = _dot_nt(q, kb_ref[pl.ds(off, tk), :]) * scale
        log_keep = _log_sigmoid(-z)
        if keep is not None:
            log_keep = jnp.where(keep, log_keep, 0.0)
        return z, log_keep, vb_ref[pl.ds(off, tk), :]

    def suffix_sums(log_keep):
        lk_hi = log_keep.astype(BF16)
        lk_lo = (log_keep - lk_hi.astype(F32)).astype(BF16)
        return _dot(lk_hi, suffix) + _dot(lk_lo, suffix)

    def live(tail):
        return (jnp.max(tail + z_bound) > -STICK_UNDERFLOW).astype(jnp.int32)

    causal = ss < jj
    has_left = qi > 0
    z_d, lk_d, v_d = logits(qi, causal)
    z_l, lk_l, v_l = logits(jnp.maximum(qi - 1, 0), has_left)
    within_d = suffix_sums(lk_d)
    within_l = suffix_sums(lk_l) + within_d[:, 0:1]
    a_d = jnp.where(causal, jnp.exp(z_d + within_d), 0.0)
    a_l = jnp.where(has_left, jnp.exp(z_l + within_l), 0.0)
    acc = _dot(a_d.astype(BF16), v_d) + _dot(a_l.astype(BF16), v_l)
    tail = within_l[:, 0:1]

    def cond(carry):
        c, go, _, _ = carry
        return jnp.logical_and(c >= 0, go > 0)

    def body(carry):
        c, _, tail, acc = carry
        z, log_keep, v = logits(c, None)
        within = suffix_sums(log_keep) + tail
        acc = acc + _dot(jnp.exp(z + within).astype(BF16), v)
        tail = within[:, 0:1]
        return c - 1, live(tail), tail, acc

    _, _, _, acc = lax.while_loop(cond, body, (qi - 2, live(tail), tail, acc))
    o_ref[...] = _rms(acc, gain_ref[...]).astype(BF16)


def _attn_b(proj, gains, layer, batch, seq, gh, col0, tq):
    nq = seq // tq
    return pl.pallas_call(
        functools.partial(_attn_b_kernel, tq=tq),
        grid=(batch, gh, nq),
        in_specs=[
            pl.BlockSpec((tq, HEAD_DIM), lambda b, h, i: (b * nq + i, col0 + h)),
            pl.BlockSpec((seq, HEAD_DIM), lambda b, h, i: (b, col0 + gh + h)),
            pl.BlockSpec((seq, HEAD_DIM), lambda b, h, i: (b, col0 + 2 * gh + h)),
            pl.BlockSpec((None, None, 1, HEAD_DIM), lambda b, h, i: (layer, gh + h, 0, 0)),
        ],
        out_specs=pl.BlockSpec((tq, HEAD_DIM), lambda b, h, i: (b * nq + i, h)),
        out_shape=jax.ShapeDtypeStruct((batch * seq, gh * HEAD_DIM), BF16),
        scratch_shapes=[pltpu.VMEM((seq, HEAD_DIM), BF16), pltpu.VMEM((seq, HEAD_DIM), BF16),
                        pltpu.VMEM((8, HEAD_DIM), F32)],
        compiler_params=pltpu.CompilerParams(
            dimension_semantics=("parallel", "parallel", "arbitrary"),
            vmem_limit_bytes=VMEM_LIMIT),
        name="stick_breaking_attention",
    )(proj, proj, proj, gains)


def _hgrn_chunk(q, k, v, b, state_t):
    c = HGRN_CHUNK
    nsub = c // HGRN_SUB
    o_inter = _dot_nt((q * jnp.exp(b)).astype(BF16), state_t.astype(BF16))
    row_id = lax.broadcasted_iota(jnp.int32, (HGRN_SUB, 1), 0)
    blocks = [o_inter[HGRN_SUB * i:HGRN_SUB * (i + 1)] for i in range(nsub)]
    for sub in range(nsub):
        lo = sub * HGRN_SUB
        hi = lo + HGRN_SUB
        q_d, b_d = q[lo:hi], b[lo:hi]
        q_r, b_r = q[hi:], b[hi:]
        diag = None
        rest = None
        for si in range(HGRN_SUB):
            s = lo + si
            b_s, k_s, v_s = b[s:s + 1], k[s:s + 1], v[s:s + 1]
            decay = jnp.exp(jnp.minimum(b_d - b_s, 0.0))
            score = jnp.sum(q_d * decay * k_s, axis=-1, keepdims=True)
            if si > 0:
                score = jnp.where(row_id >= si, score, 0.0)
            term = score * v_s
            diag = term if diag is None else diag + term
            if sub < nsub - 1:
                score = jnp.sum(q_r * jnp.exp(b_r - b_s) * k_s, axis=-1, keepdims=True)
                term = score * v_s
                rest = term if rest is None else rest + term
        blocks[sub] = blocks[sub] + diag
        for i in range(sub + 1, nsub):
            r = (i - sub - 1) * HGRN_SUB
            blocks[i] = blocks[i] + rest[r:r + HGRN_SUB]
    o = jnp.concatenate(blocks, axis=0)
    b_last = b[c - 1:c]
    k_decayed = k * jnp.exp(b_last - b)
    new_state_t = jnp.exp(b_last) * state_t + _dot_tn(v.astype(BF16), k_decayed.astype(BF16))
    return o, new_state_t


def _hgrn_kernel(cq_ref, cf_ref, ci_ref, cg_ref, lb_ref, gain_ref, o_ref, state_ref, of_ref, *, gh):
    ts = cq_ref.shape[0]
    c = HGRN_CHUNK

    @pl.when(pl.program_id(1) == 0)
    def _():
        state_ref[...] = jnp.zeros_like(state_ref)

    lb = lb_ref[...]
    log_lb = jnp.log(jnp.maximum(lb, LB_FLOOR))
    log_1m_lb = jnp.log1p(-lb)
    ii = lax.broadcasted_iota(jnp.int32, (c, c), 0)
    jj = lax.broadcasted_iota(jnp.int32, (c, c), 1)
    prefix = jnp.where(jj <= ii, 1.0, 0.0).astype(BF16)

    def chunk(ci, _):
        base = pl.multiple_of(ci * c, c)
        rows = pl.ds(base, c)
        cf = cf_ref[rows, :]
        x = log_1m_lb + _log_sigmoid(cf)
        hi = jnp.maximum(log_lb, x)
        log_f = hi + jnp.log1p(jnp.exp(-jnp.abs(log_lb - x)))
        b_all = _dot_exact_lhs(prefix, log_f)
        k_all = (1.0 - lb) * _sigmoid(-cf)
        q_all = _silu(cq_ref[rows, :])
        v_all = ci_ref[rows, :]
        for h in range(gh):
            cols = slice(h * HEAD_DIM, (h + 1) * HEAD_DIM)
            o, st = _hgrn_chunk(q_all[:, cols], k_all[:, cols], v_all[:, cols], b_all[:, cols],
                                state_ref[h])
            state_ref[h] = st
            of_ref[rows, cols] = o
        return 0

    lax.fori_loop(0, ts // c, chunk, 0, unroll=2)

    for h in range(gh):
        cols = slice(h * HEAD_DIM, (h + 1) * HEAD_DIM)
        o = _rms(of_ref[:, cols], gain_ref[:, cols]) * _silu(cg_ref[:, cols])
        o_ref[:, cols] = o.astype(BF16)


def _hgrn(proj, lower_bound, gains, layer, batch, seq, gh, col0, ts):
    gw = gh * HEAD_DIM
    nt = seq // ts
    blk = lambda k: pl.BlockSpec((ts, gw), lambda b, i, k=k: (b * nt + i, col0 + k))
    return pl.pallas_call(
        functools.partial(_hgrn_kernel, gh=gh),
        grid=(batch, nt),
        in_specs=[
            blk(0), blk(1), blk(2), blk(3),
            pl.BlockSpec((None, 1, gw), lambda b, i: (layer, 0, 0)),
            pl.BlockSpec((None, None, 1, gw), lambda b, i: (layer, 2, 0, 0)),
        ],
        out_specs=pl.BlockSpec((ts, gw), lambda b, i: (b * nt + i, 0)),
        out_shape=jax.ShapeDtypeStruct((batch * seq, gw), BF16),
        scratch_shapes=[pltpu.VMEM((gh, HEAD_DIM, HEAD_DIM), F32), pltpu.VMEM((ts, gw), F32)],
        compiler_params=pltpu.CompilerParams(
            dimension_semantics=("parallel", "arbitrary"), vmem_limit_bytes=VMEM_LIMIT),
        name="hgrn2",
    )(proj, proj, proj, proj, lower_bound, gains)


def _gdn_kernel(xq_ref, xk_ref, xv_ref, z_ref, ab_ref, wq_ref, wk_ref, wv_ref, alog_ref, dt_ref,
                gain_ref, o_ref, pad_ref, act_ref, state_ref, of_ref, *, gh):
    ts = xq_ref.shape[0]
    c = GDN_CHUNK
    first = pl.program_id(1) == 0

    @pl.when(first)
    def _():
        state_ref[...] = jnp.zeros_like(state_ref)
        pad_ref[:, 0:CONV_HALO, :] = jnp.zeros((3, CONV_HALO, pad_ref.shape[2]), F32)

    @pl.when(jnp.logical_not(first))
    def _():
        pad_ref[:, 0:CONV_HALO, :] = pad_ref[:, ts:ts + CONV_HALO, :]

    for part, (x_ref, w_ref) in enumerate(((xq_ref, wq_ref), (xk_ref, wk_ref), (xv_ref, wv_ref))):
        pad_ref[part, CONV_HALO:CONV_HALO + ts, :] = x_ref[...]
        acc = None
        for tap in range(CONV_WIDTH):
            start = CONV_HALO - (CONV_WIDTH - 1) + tap
            term = w_ref[tap:tap + 1, :] * pad_ref[part, start:start + ts, :]
            acc = term if acc is None else acc + term
        act_ref[part] = _silu(acc)

    ab = ab_ref[:, 0:HEAD_DIM]
    log_alpha = -jnp.exp(alog_ref[...]) * _softplus(ab + dt_ref[...])
    beta_all = _sigmoid(ab)

    ii = lax.broadcasted_iota(jnp.int32, (c, c), 0)
    jj = lax.broadcasted_iota(jnp.int32, (c, c), 1)
    incl = jj <= ii
    strict = jj < ii
    prefix = jnp.where(incl, 1.0, 0.0).astype(BF16)
    base_shift = int(np.log2(GDN_INV_BASE))
    in_base = jnp.where((ii >> base_shift) == (jj >> base_shift), 1.0, 0.0)
    merge_masks = []
    for shift in range(base_shift, int(np.log2(c))):
        same_pair = (ii >> (shift + 1)) == (jj >> (shift + 1))
        other_half = (ii >> shift) != (jj >> shift)
        merge_masks.append(jnp.where(same_pair & other_half, 1.0, 0.0))

    n_chunks = ts // c
    bodies = [(ci, h) for ci in range(n_chunks) for h in range(gh)]
    g_all, g_rows = [], []
    for ci in range(n_chunks):
        g = _dot_exact_lhs(prefix, log_alpha[ci * c:(ci + 1) * c])
        g_all.append(g)
        g_rows.append(g.T)

    def load(part, ci, h):
        return act_ref[part, ci * c:(ci + 1) * c, h * HEAD_DIM:(h + 1) * HEAD_DIM]

    def l2norm(t):
        return t * lax.rsqrt(jnp.sum(t * t, axis=-1, keepdims=True) + EPS)

    q = [l2norm(load(0, ci, h)) * (HEAD_DIM ** -0.5) for ci, h in bodies]
    k = [l2norm(load(1, ci, h)) for ci, h in bodies]
    v = [load(2, ci, h) for ci, h in bodies]
    beta = [beta_all[ci * c:(ci + 1) * c, gh + h:gh + h + 1] for ci, h in bodies]
    g_col = [g_all[ci][:, h:h + 1] for ci, h in bodies]
    g_last = [g_all[ci][c - 1:c, h:h + 1] for ci, h in bodies]
    decay = [jnp.where(incl, jnp.exp(jnp.minimum(gc - g_rows[ci][h:h + 1, :], 0.0)), 0.0)
             for gc, (ci, h) in zip(g_col, bodies)]
    k_beta = [ki * bi for ki, bi in zip(k, beta)]
    k16 = [ki.astype(BF16) for ki in k]
    a = [jnp.where(strict, _dot_nt(kb.astype(BF16), kh) * dc, 0.0)
         for kb, kh, dc in zip(k_beta, k16, decay)]
    power = [ai * in_base for ai in a]
    n = [-p for p in power]
    for _ in range(base_shift - 1):
        power = [_dot16(p, p) for p in power]
        n = [ni + p + _dot16(ni, p) for ni, p in zip(n, power)]
    for mask in merge_masks:
        off = [ai * mask for ai in a]
        x = [o + _dot16(ni, o) for ni, o in zip(n, off)]
        n = [ni - (xi + _dot16(xi, ni)) for ni, xi in zip(n, x)]
    rhs = [jnp.concatenate([vi * bi, kb * jnp.exp(gc)], axis=1)
           for vi, bi, kb, gc in zip(v, beta, k_beta, g_col)]
    uw = [r + _dot16(ni, r) for ni, r in zip(n, rhs)]
    qk = [(_dot_nt(qi.astype(BF16), kh) * dc).astype(BF16) for qi, kh, dc in zip(q, k16, decay)]
    q_in = [(qi * jnp.exp(gc)).astype(BF16) for qi, gc in zip(q, g_col)]
    k_out = [(ki * jnp.exp(gl - gc)).astype(BF16) for ki, gl, gc in zip(k, g_last, g_col)]

    for i, (ci, h) in enumerate(bodies):
        state = state_ref[h]
        s16 = state.astype(BF16)
        v_new = uw[i][:, :HEAD_DIM] - _dot(uw[i][:, HEAD_DIM:].astype(BF16), s16)
        vn16 = v_new.astype(BF16)
        o = _dot(q_in[i], s16) + _dot(qk[i], vn16)
        state_ref[h] = jnp.exp(g_last[i]) * state + _dot_tn(k_out[i], vn16)
        of_ref[ci * c:(ci + 1) * c, h * HEAD_DIM:(h + 1) * HEAD_DIM] = o

    for h in range(gh):
        cols = slice(h * HEAD_DIM, (h + 1) * HEAD_DIM)
        o = _rms(of_ref[:, cols], gain_ref[:, cols]) * _silu(z_ref[:, cols])
        o_ref[:, cols] = o.astype(BF16)


def _gdn(proj, conv_w, alog_row, dt_row, gains, layer, batch, seq, gh, col0, ts):
    gw = gh * HEAD_DIM
    nt = seq // ts
    blk = lambda k: pl.BlockSpec((ts, gw), lambda b, i, k=k: (b * nt + i, col0 + k))
    wblk = lambda k: pl.BlockSpec((None, CONV_WIDTH, gw), lambda b, i, k=k: (layer, 0, k))
    row = pl.BlockSpec((None, 1, HEAD_DIM), lambda b, i: (layer, 0, 0))
    return pl.pallas_call(
        functools.partial(_gdn_kernel, gh=gh),
        grid=(batch, nt),
        in_specs=[
            blk(0), blk(1), blk(2), blk(3), blk(4),
            wblk(0), wblk(1), wblk(2), row, row,
            pl.BlockSpec((None, None, 1, gw), lambda b, i: (layer, 3, 0, 0)),
        ],
        out_specs=pl.BlockSpec((ts, gw), lambda b, i: (b * nt + i, 0)),
        out_shape=jax.ShapeDtypeStruct((batch * seq, gw), BF16),
        scratch_shapes=[
            pltpu.VMEM((3, ts + CONV_HALO, gw), F32),
            pltpu.VMEM((3, ts, gw), F32),
            pltpu.VMEM((gh, HEAD_DIM, HEAD_DIM), F32),
            pltpu.VMEM((ts, gw), F32),
        ],
        compiler_params=pltpu.CompilerParams(
            dimension_semantics=("parallel", "arbitrary"), vmem_limit_bytes=VMEM_LIMIT),
        name="gated_deltanet",
    )(proj, proj, proj, proj, proj, conv_w, conv_w, conv_w, alog_row, dt_row, gains)


def _outproj_kernel(x_ref, oa_ref, ob_ref, oc_ref, od_ref, w_ref, gt_ref, gain_ref, o_ref):
    gw = oa_ref.shape[1]
    y = None
    for g, m_ref in enumerate((oa_ref, ob_ref, oc_ref, od_ref)):
        part = _dot(m_ref[...], w_ref[g * gw:(g + 1) * gw, :])
        y = part if y is None else y + part
    o_ref[...] = x_ref[...] + gt_ref[...] * _rms(y, gain_ref[...])


def _outproj(x, groups, w_out, gate, gain, layer, tiles_per_batch, tm):
    n, d = x.shape
    gw = groups[0].shape[1]
    gspec = pl.BlockSpec((tm, gw), lambda i: (i, 0))
    return pl.pallas_call(
        _outproj_kernel,
        grid=(n // tm,),
        in_specs=[
            pl.BlockSpec((tm, d), lambda i: (i, 0)),
            gspec, gspec, gspec, gspec,
            pl.BlockSpec((None, N_GROUPS * gw, d), lambda i: (layer, 0, 0)),
            pl.BlockSpec((None, 1, d), lambda i: (i // tiles_per_batch, 0, 0)),
            pl.BlockSpec((1, d), lambda i: (0, 0)),
        ],
        out_specs=pl.BlockSpec((tm, d), lambda i: (i, 0)),
        out_shape=jax.ShapeDtypeStruct((n, d), F32),
        compiler_params=pltpu.CompilerParams(
            dimension_semantics=("parallel",), vmem_limit_bytes=VMEM_LIMIT),
        name="outproj",
    )(x, *groups, w_out, gate, gain)


def _largest_tile(n, cap):
    t = min(n, cap)
    while n % t:
        t //= 2
    return t


def kernel(x, c, w_mod, b_mod, norm_gain, w_in, w_out, mix_out_gain, hgrn_lb_logits, dn_conv_w,
           dn_a_log, dn_dt_bias, ffn1_w13, ffn1_w2, ffn2_w13, ffn2_w2):
    batch, seq, d = x.shape
    depth = w_mod.shape[0]
    gw = d // N_GROUPS
    gh = gw // HEAD_DIM
    dff = ffn1_w2.shape[1]
    n = batch * seq

    tm_ffn = _largest_tile(seq, 512)
    tf = dff // 11 if dff % 11 == 0 and (dff // 11) % 128 == 0 else _largest_tile(dff, 512)
    tm_proj = _largest_tile(seq, 1024)
    tm_out = _largest_tile(seq, 512)
    tq = _largest_tile(seq, 256)
    ts_c = _largest_tile(seq, 1024)
    ts_d = _largest_tile(seq, 512)

    in_cols = w_in.shape[2]
    pad_cols = (-in_cols) % gw
    w_in16 = jnp.pad(w_in, ((0, 0), (0, 0), (0, pad_cols))).astype(BF16)
    w_out16 = w_out.astype(BF16)
    ffn_w = [(ffn1_w13.astype(BF16), ffn1_w2.astype(BF16)),
             (ffn2_w13.astype(BF16), ffn2_w2.astype(BF16))]
    lb_p = jax.nn.softmax(hgrn_lb_logits.astype(F32), axis=0)
    lower_bounds = (jnp.cumsum(lb_p, axis=0) - lb_p[0]).reshape(depth, 1, gw)
    gains = mix_out_gain.reshape(depth, N_GROUPS, 1, gw)
    head_gains = mix_out_gain.reshape(depth, N_GROUPS * gh, 1, HEAD_DIM)
    lane_pad = ((0, 0), (0, HEAD_DIM - gh))
    alog_row = jnp.pad(dn_a_log.astype(F32), lane_pad).reshape(depth, 1, HEAD_DIM)
    dt_row = jnp.pad(dn_dt_bias.astype(F32), lane_pad).reshape(depth, 1, HEAD_DIM)
    conv_w = dn_conv_w.astype(F32)

    c_rows = -(-batch // 8) * 8
    c_pad = jnp.pad(c, ((0, c_rows - batch), (0, 0)))
    mod = _modulation(c_pad, w_mod, b_mod, _largest_tile(N_MOD * d, 2048))[:, :batch]
    mod = mod.reshape(depth, batch, N_MOD, 1, d)

    xf = x.reshape(n, d)
    for l in range(depth):
        sh1, sc1, g1, sh2, sc2, g2, sh3, sc3, g3 = (mod[l, :, i] for i in range(N_MOD))
        ng = norm_gain[l].reshape(6, 1, d)
        xf = _ffn(xf, sh1, sc1, g1, ng[0], ng[1], *ffn_w[0], l, seq // tm_ffn, tm_ffn, tf)
        proj = _inproj(xf, sh2, sc2, ng[2], w_in16, l, seq // tm_proj, tm_proj, gw)
        hb = gw // HEAD_DIM
        o_a = _attn_a(proj, head_gains, l, batch, seq, gh, 0 * hb, tq, tq)
        o_b = _attn_b(proj, head_gains, l, batch, seq, gh, 3 * hb, tq)
        o_c = _hgrn(proj, lower_bounds, gains, l, batch, seq, gh, 6, ts_c)
        o_d = _gdn(proj, conv_w, alog_row, dt_row, gains, l, batch, seq, gh, 10, ts_d)
        xf = _outproj(xf, (o_a, o_b, o_c, o_d), w_out16, g2, ng[3], l, seq // tm_out, tm_out)
        xf = _ffn(xf, sh3, sc3, g3, ng[4], ng[5], *ffn_w[1], l, seq // tm_ffn, tm_ffn, tf)
    return xf.reshape(batch, seq, d)
```

```python
import functools

import jax
import jax.numpy as jnp
import numpy as np
from jax import lax
from jax.experimental import pallas as pl
from jax.experimental.pallas import tpu as pltpu

F32 = jnp.float32
BF16 = jnp.bfloat16

HEAD_DIM = 128
N_GROUPS = 4
N_MOD = 9
CONV_WIDTH = 4
DILATED_BRANCHES = ((128, 1), (512, 4), (2048, 16))
DILATED_GROUP = 2
EPS = 1e-6
NEG_BIG = -1e30
LB_FLOOR = 1e-30

HGRN_CHUNK = 32
HGRN_SUB = 8
GDN_CHUNK = 128
GDN_INV_BASE = 8
CONV_HALO = 8
NORM_ROWS = 128
VMEM_LIMIT = 56 * 1024 * 1024
SINGLE_BUFFER = pl.Buffered(1)

_NT = (((1,), (1,)), ((), ()))
_TN = (((0,), (0,)), ((), ()))


def _dot(a, b):
    return jnp.dot(a, b, preferred_element_type=F32)


def _dot_nt(a, b):
    return lax.dot_general(a, b, _NT, preferred_element_type=F32)


def _dot_tn(a, b):
    return lax.dot_general(a, b, _TN, preferred_element_type=F32)


def _split3(a):
    hi = a.astype(BF16)
    r = a - hi.astype(F32)
    mid = r.astype(BF16)
    lo = (r - mid.astype(F32)).astype(BF16)
    return hi, mid, lo


def _dot_exact_lhs(a_bf16, b):
    hi, mid, lo = _split3(b)
    return _dot(a_bf16, hi) + (_dot(a_bf16, mid) + _dot(a_bf16, lo))


def _dot16(a, b):
    return _dot(a.astype(BF16), b.astype(BF16))


def _sigmoid(x):
    return 1.0 / (1.0 + jnp.exp(-x))


def _silu(x):
    return x * _sigmoid(x)


def _softplus(x):
    return jnp.maximum(x, 0.0) + jnp.log1p(jnp.exp(-jnp.abs(x)))


def _log_sigmoid(x):
    return jnp.minimum(x, 0.0) - jnp.log1p(jnp.exp(-jnp.abs(x)))


def _rms(x, gain):
    return x * lax.rsqrt(jnp.mean(x * x, axis=-1, keepdims=True) + EPS) * gain


def _mod_kernel(c_ref, w_ref, b_ref, o_ref):
    cond = _silu(c_ref[...]).astype(BF16)
    o_ref[...] = _dot(cond, w_ref[...].astype(BF16)) + b_ref[...]


def _modulation(c_pad, w_mod, b_mod, tn):
    depth, d, n = w_mod.shape
    rows = c_pad.shape[0]
    return pl.pallas_call(
        _mod_kernel,
        grid=(depth, n // tn),
        in_specs=[
            pl.BlockSpec((rows, d), lambda l, j: (0, 0)),
            pl.BlockSpec((None, d, tn), lambda l, j: (l, 0, j)),
            pl.BlockSpec((None, 1, tn), lambda l, j: (l, 0, j)),
        ],
        out_specs=pl.BlockSpec((None, rows, tn), lambda l, j: (l, 0, j)),
        out_shape=jax.ShapeDtypeStruct((depth, rows, n), F32),
        compiler_params=pltpu.CompilerParams(
            dimension_semantics=("parallel", "parallel"), vmem_limit_bytes=VMEM_LIMIT),
        name="modulation",
    )(c_pad, w_mod, b_mod.reshape(depth, 1, n))


def _norm_modulate(x, gain, shift, scale):
    return (_rms(x, gain) * (1.0 + scale) + shift).astype(BF16)


def _for_row_slices(n_rows, body):
    def step(r, carry):
        body(pl.ds(pl.multiple_of(r * NORM_ROWS, NORM_ROWS), NORM_ROWS))
        return carry

    lax.fori_loop(0, n_rows // NORM_ROWS, step, 0)


def _inv_rms(x):
    return lax.rsqrt(jnp.mean(x * x, axis=-1, keepdims=True) + EPS)


def _norm_modulate_tile(x_ref, r_ref, ga_ref, sh_ref, sc_ref, h_ref, also=None):
    r_ref[...] = _inv_rms(x_ref[...])

    def body(rows):
        y = x_ref[rows, :] * r_ref[rows, :] * ga_ref[...]
        h_ref[rows, :] = (y * (1.0 + sc_ref[...]) + sh_ref[...]).astype(BF16)
        if also is not None:
            also(rows)

    _for_row_slices(x_ref.shape[0], body)


def _ffn_kernel(x_ref, sh_ref, sc_ref, gt_ref, ga_ref, gb_ref, w1_ref, w3_ref, w2_ref,
                o_ref, h_ref, acc_ref, r_ref):
    j = pl.program_id(1)

    @pl.when(j == 0)
    def _():
        def clear(rows):
            acc_ref[rows, :] = jnp.zeros((NORM_ROWS, acc_ref.shape[1]), F32)

        _norm_modulate_tile(x_ref, r_ref, ga_ref, sh_ref, sc_ref, h_ref, also=clear)

    h = h_ref[...]
    gate = _dot(h, w1_ref[...])
    up = _dot(h, w3_ref[...])
    act = (_silu(gate) * up).astype(BF16)
    acc_ref[...] += _dot(act, w2_ref[...])

    @pl.when(j == pl.num_programs(1) - 1)
    def _():
        r_ref[...] = _inv_rms(acc_ref[...])

        def body(rows):
            y = acc_ref[rows, :] * r_ref[rows, :] * gb_ref[...]
            o_ref[rows, :] = x_ref[rows, :] + (0.5 * gt_ref[...]) * y

        _for_row_slices(x_ref.shape[0], body)


def _ffn(x, shift, scale, gate, gain_a, gain_b, w13, w2, layer, tiles_per_batch, tm, tf):
    n, d = x.shape
    dff = w2.shape[1]
    nf = dff // tf
    row = lambda i, j: (i, 0)
    per_batch = lambda i, j: (i // tiles_per_batch, 0, 0)
    const = lambda i, j: (0, 0)
    return pl.pallas_call(
        _ffn_kernel,
        grid=(n // tm, nf),
        in_specs=[
            pl.BlockSpec((tm, d), row),
            pl.BlockSpec((None, 1, d), per_batch),
            pl.BlockSpec((None, 1, d), per_batch),
            pl.BlockSpec((None, 1, d), per_batch),
            pl.BlockSpec((1, d), const),
            pl.BlockSpec((1, d), const),
            pl.BlockSpec((None, d, tf), lambda i, j: (layer, 0, j)),
            pl.BlockSpec((None, d, tf), lambda i, j: (layer, 0, nf + j)),
            pl.BlockSpec((None, tf, d), lambda i, j: (layer, j, 0)),
        ],
        out_specs=pl.BlockSpec((tm, d), row, pipeline_mode=SINGLE_BUFFER),
        out_shape=jax.ShapeDtypeStruct((n, d), F32),
        scratch_shapes=[pltpu.VMEM((tm, d), BF16), pltpu.VMEM((tm, d), F32),
                        pltpu.VMEM((tm, 1), F32)],
        compiler_params=pltpu.CompilerParams(
            dimension_semantics=("parallel", "arbitrary"), vmem_limit_bytes=VMEM_LIMIT),
        name="ffn",
    )(x, shift, scale, gate, gain_a, gain_b, w13, w13, w2)


def _inproj_kernel(x_ref, sh_ref, sc_ref, ga_ref, w_ref, o_ref, h_ref, r_ref):
    @pl.when(pl.program_id(1) == 0)
    def _():
        _norm_modulate_tile(x_ref, r_ref, ga_ref, sh_ref, sc_ref, h_ref)

    o_ref[...] = _dot(h_ref[...], w_ref[...])


def _inproj(x, shift, scale, gain, w_in, layer, tiles_per_batch, tm, tn):
    n, d = x.shape
    cols = w_in.shape[2]
    per_batch = lambda i, j: (i // tiles_per_batch, 0, 0)
    return pl.pallas_call(
        _inproj_kernel,
        grid=(n // tm, cols // tn),
        in_specs=[
            pl.BlockSpec((tm, d), lambda i, j: (i, 0)),
            pl.BlockSpec((None, 1, d), per_batch),
            pl.BlockSpec((None, 1, d), per_batch),
            pl.BlockSpec((1, d), lambda i, j: (0, 0)),
            pl.BlockSpec((None, d, tn), lambda i, j: (layer, 0, j)),
        ],
        out_specs=pl.BlockSpec((tm, tn), lambda i, j: (i, j)),
        out_shape=jax.ShapeDtypeStruct((n, cols), F32),
        scratch_shapes=[pltpu.VMEM((tm, d), BF16), pltpu.VMEM((tm, 1), F32)],
        compiler_params=pltpu.CompilerParams(
            dimension_semantics=("parallel", "arbitrary"), vmem_limit_bytes=VMEM_LIMIT),
        name="inproj",
    )(x, shift, scale, gain, w_in)


def _dilated_log_counts(tq, tk):
    reach = max(w for w, _ in DILATED_BRANCHES)
    nd = (reach + tq - 1) // tk + 1
    d = np.arange(nd)[:, None, None]
    i = np.arange(tq)[None, :, None]
    j = np.arange(tk)[None, None, :]
    delta = d * tk + i - j
    cnt = np.zeros(delta.shape, np.float64)
    for window, dil in DILATED_BRANCHES:
        cnt += (delta >= 0) & (delta <= window) & (delta % dil == 0)
    table = np.full((nd + 1, tq, tk), NEG_BIG, np.float32)
    table[:nd] = np.where(cnt > 0, np.log(np.maximum(cnt, 1.0)), NEG_BIG)
    return table


def _attn_a_kernel(q_ref, k_ref, v_ref, lc_ref, gain_ref, o_ref, kb_ref, vb_ref, s_ref, *, tq, tk, nd):
    qi = pl.program_id(2)

    @pl.when(qi == 0)
    def _():
        kb_ref[...] = k_ref[...].astype(BF16)
        vb_ref[:, 0:HEAD_DIM] = v_ref[...].astype(BF16)
        vb_ref[:, HEAD_DIM:] = jnp.ones((vb_ref.shape[0], HEAD_DIM), BF16)

    q = q_ref[...].astype(BF16)
    scale = HEAD_DIM ** -0.5
    half = tk // 2

    maxes, accs = [], []
    for g0 in range(0, nd, DILATED_GROUP):
        group = range(g0, min(g0 + DILATED_GROUP, nd))
        m_run = jnp.full((tq, half), NEG_BIG, F32)
        offsets = {}
        for d in group:
            c = qi - d
            offsets[d] = pl.multiple_of(jnp.maximum(c, 0) * tk, tk)
            t = (_dot_nt(q, kb_ref[pl.ds(offsets[d], tk), :]) * scale
                 + lc_ref[jnp.where(c >= 0, d, nd)])
            s_ref[d] = t
            m_run = jnp.maximum(m_run, jnp.maximum(t[:, :half], t[:, half:]))
        m = jnp.max(m_run, axis=-1, keepdims=True)
        acc = jnp.zeros((tq, 2 * HEAD_DIM), F32)
        for d in group:
            p = jnp.exp(s_ref[d] - m)
            acc = acc + _dot(p.astype(BF16), vb_ref[pl.ds(offsets[d], tk), :])
        maxes.append(m)
        accs.append(acc)
    m_all = functools.reduce(jnp.maximum, maxes)
    acc = functools.reduce(lambda a, b: a + b,
                           [jnp.exp(m - m_all) * a for m, a in zip(maxes, accs)])
    o = acc[:, :HEAD_DIM] / acc[:, HEAD_DIM:]
    o_ref[...] = _rms(o, gain_ref[...]).astype(BF16)


def _attn_a(proj, gains, layer, batch, seq, gh, col0, tq, tk):
    table = jnp.asarray(_dilated_log_counts(tq, tk))
    nd = table.shape[0] - 1
    nq = seq // tq
    return pl.pallas_call(
        functools.partial(_attn_a_kernel, tq=tq, tk=tk, nd=nd),
        grid=(batch, gh, nq),
        in_specs=[
            pl.BlockSpec((tq, HEAD_DIM), lambda b, h, i: (b * nq + i, col0 + h)),
            pl.BlockSpec((seq, HEAD_DIM), lambda b, h, i: (b, col0 + gh + h)),
            pl.BlockSpec((seq, HEAD_DIM), lambda b, h, i: (b, col0 + 2 * gh + h)),
            pl.BlockSpec((nd + 1, tq, tk), lambda b, h, i: (0, 0, 0)),
            pl.BlockSpec((None, None, 1, HEAD_DIM), lambda b, h, i: (layer, h, 0, 0)),
        ],
        out_specs=pl.BlockSpec((tq, HEAD_DIM), lambda b, h, i: (b * nq + i, h)),
        out_shape=jax.ShapeDtypeStruct((batch * seq, gh * HEAD_DIM), BF16),
        scratch_shapes=[pltpu.VMEM((seq, HEAD_DIM), BF16), pltpu.VMEM((seq, 2 * HEAD_DIM), BF16),
                        pltpu.VMEM((nd, tq, tk), F32)],
        compiler_params=pltpu.CompilerParams(
            dimension_semantics=("parallel", "parallel", "arbitrary"),
            vmem_limit_bytes=VMEM_LIMIT),
        name="dilated_attention",
    )(proj, proj, proj, table, gains)


STICK_UNDERFLOW = 100.0
STICK_BOUND_SLACK = 1.01


def _attn_b_kernel(q_ref, k_ref, v_ref, gain_ref, o_ref, kb_ref, vb_ref, kmax_ref, *, tq):
    qi = pl.program_id(2)
    tk = tq

    @pl.when(qi == 0)
    def _():
        k = k_ref[...]
        kb_ref[...] = k.astype(BF16)
        vb_ref[...] = v_ref[...].astype(BF16)
        k_sq = jnp.max(jnp.sum(k * k, axis=-1, keepdims=True), axis=0, keepdims=True)
        kmax_ref[...] = jnp.broadcast_to(jnp.sqrt(k_sq), kmax_ref.shape)

    qf = q_ref[...]
    q = qf.astype(BF16)
    scale = HEAD_DIM ** -0.5
    q_norm = jnp.sqrt(jnp.sum(qf * qf, axis=-1, keepdims=True))
    z_bound = (scale * STICK_BOUND_SLACK) * q_norm * kmax_ref[0:1, 0:1]
    jj = lax.broadcasted_iota(jnp.int32, (tk, tk), 0)
    ss = lax.broadcasted_iota(jnp.int32, (tk, tk), 1)
    suffix = jnp.where(jj >= ss, 1.0, 0.0).astype(BF16)

    def logits(c, keep):
        off = pl.multiple_of(c * tk, tk)
        z = _dot_nt(q, kb_ref[pl.ds(off, tk), :]) * scale
        log_keep = _log_sigmoid(-z)
        if keep is not None:
            log_keep = jnp.where(keep, log_keep, 0.0)
        return z, log_keep, vb_ref[pl.ds(off, tk), :]

    def suffix_sums(log_keep):
        lk_hi = log_keep.astype(BF16)
        lk_lo = (log_keep - lk_hi.astype(F32)).astype(BF16)
        return _dot(lk_hi, suffix) + _dot(lk_lo, suffix)

    def live(tail):
        return (jnp.max(tail + z_bound) > -STICK_UNDERFLOW).astype(jnp.int32)

    causal = ss < jj
    has_left = qi > 0
    z_d, lk_d, v_d = logits(qi, causal)
    z_l, lk_l, v_l = logits(jnp.maximum(qi - 1, 0), has_left)
    within_d = suffix_sums(lk_d)
    within_l = suffix_sums(lk_l) + within_d[:, 0:1]
    a_d = jnp.where(causal, jnp.exp(z_d + within_d), 0.0)
    a_l = jnp.where(has_left, jnp.exp(z_l + within_l), 0.0)
    acc = _dot(a_d.astype(BF16), v_d) + _dot(a_l.astype(BF16), v_l)
    tail = within_l[:, 0:1]

    def cond(carry):
        c, go, _, _ = carry
        return jnp.logical_and(c >= 0, go > 0)

    def body(carry):
        c, _, tail, acc = carry
        z, log_keep, v = logits(c, None)
        within = suffix_sums(log_keep) + tail
        acc = acc + _dot(jnp.exp(z + within).astype(BF16), v)
        tail = within[:, 0:1]
        return c - 1, live(tail), tail, acc

    _, _, _, acc = lax.while_loop(cond, body, (qi - 2, live(tail), tail, acc))
    o_ref[...] = _rms(acc, gain_ref[...]).astype(BF16)


def _attn_b(proj, gains, layer, batch, seq, gh, col0, tq):
    nq = seq // tq
    return pl.pallas_call(
        functools.partial(_attn_b_kernel, tq=tq),
        grid=(batch, gh, nq),
        in_specs=[
            pl.BlockSpec((tq, HEAD_DIM), lambda b, h, i: (b * nq + i, col0 + h)),
            pl.BlockSpec((seq, HEAD_DIM), lambda b, h, i: (b, col0 + gh + h)),
            pl.BlockSpec((seq, HEAD_DIM), lambda b, h, i: (b, col0 + 2 * gh + h)),
            pl.BlockSpec((None, None, 1, HEAD_DIM), lambda b, h, i: (layer, gh + h, 0, 0)),
        ],
        out_specs=pl.BlockSpec((tq, HEAD_DIM), lambda b, h, i: (b * nq + i, h)),
        out_shape=jax.ShapeDtypeStruct((batch * seq, gh * HEAD_DIM), BF16),
        scratch_shapes=[pltpu.VMEM((seq, HEAD_DIM), BF16), pltpu.VMEM((seq, HEAD_DIM), BF16),
                        pltpu.VMEM((8, HEAD_DIM), F32)],
        compiler_params=pltpu.CompilerParams(
            dimension_semantics=("parallel", "parallel", "arbitrary"),
            vmem_limit_bytes=VMEM_LIMIT),
        name="stick_breaking_attention",
    )(proj, proj, proj, gains)


def _hgrn_chunk(q, k, v, b, state_t):
    c = HGRN_CHUNK
    nsub = c // HGRN_SUB
    o_inter = _dot_nt((q * jnp.exp(b)).astype(BF16), state_t.astype(BF16))
    row_id = lax.broadcasted_iota(jnp.int32, (HGRN_SUB, 1), 0)
    blocks = [o_inter[HGRN_SUB * i:HGRN_SUB * (i + 1)] for i in range(nsub)]
    for sub in range(nsub):
        lo = sub * HGRN_SUB
        hi = lo + HGRN_SUB
        q_d, b_d = q[lo:hi], b[lo:hi]
        q_r, b_r = q[hi:], b[hi:]
        diag = None
        rest = None
        for si in range(HGRN_SUB):
            s = lo + si
            b_s, k_s, v_s = b[s:s + 1], k[s:s + 1], v[s:s + 1]
            decay = jnp.exp(jnp.minimum(b_d - b_s, 0.0))
            score = jnp.sum(q_d * decay * k_s, axis=-1, keepdims=True)
            if si > 0:
                score = jnp.where(row_id >= si, score, 0.0)
            term = score * v_s
            diag = term if diag is None else diag + term
            if sub < nsub - 1:
                score = jnp.sum(q_r * jnp.exp(b_r - b_s) * k_s, axis=-1, keepdims=True)
                term = score * v_s
                rest = term if rest is None else rest + term
        blocks[sub] = blocks[sub] + diag
        for i in range(sub + 1, nsub):
            r = (i - sub - 1) * HGRN_SUB
            blocks[i] = blocks[i] + rest[r:r + HGRN_SUB]
    o = jnp.concatenate(blocks, axis=0)
    b_last = b[c - 1:c]
    k_decayed = k * jnp.exp(b_last - b)
    new_state_t = jnp.exp(b_last) * state_t + _dot_tn(v.astype(BF16), k_decayed.astype(BF16))
    return o, new_state_t


def _hgrn_kernel(cq_ref, cf_ref, ci_ref, cg_ref, lb_ref, gain_ref, o_ref, state_ref, of_ref, *, gh):
    ts = cq_ref.shape[0]
    c = HGRN_CHUNK

    @pl.when(pl.program_id(1) == 0)
    def _():
        state_ref[...] = jnp.zeros_like(state_ref)

    lb = lb_ref[...]
    log_lb = jnp.log(jnp.maximum(lb, LB_FLOOR))
    log_1m_lb = jnp.log1p(-lb)
    ii = lax.broadcasted_iota(jnp.int32, (c, c), 0)
    jj = lax.broadcasted_iota(jnp.int32, (c, c), 1)
    prefix = jnp.where(jj <= ii, 1.0, 0.0).astype(BF16)

    def chunk(ci, _):
        base = pl.multiple_of(ci * c, c)
        rows = pl.ds(base, c)
        cf = cf_ref[rows, :]
        x = log_1m_lb + _log_sigmoid(cf)
        hi = jnp.maximum(log_lb, x)
        log_f = hi + jnp.log1p(jnp.exp(-jnp.abs(log_lb - x)))
        b_all = _dot_exact_lhs(prefix, log_f)
        k_all = (1.0 - lb) * _sigmoid(-cf)
        q_all = _silu(cq_ref[rows, :])
        v_all = ci_ref[rows, :]
        for h in range(gh):
            cols = slice(h * HEAD_DIM, (h + 1) * HEAD_DIM)
            o, st = _hgrn_chunk(q_all[:, cols], k_all[:, cols], v_all[:, cols], b_all[:, cols],
                                state_ref[h])
            state_ref[h] = st
            of_ref[rows, cols] = o
        return 0

    lax.fori_loop(0, ts // c, chunk, 0, unroll=2)

    for h in range(gh):
        cols = slice(h * HEAD_DIM, (h + 1) * HEAD_DIM)
        o = _rms(of_ref[:, cols], gain_ref[:, cols]) * _silu(cg_ref[:, cols])
        o_ref[:, cols] = o.astype(BF16)


def _hgrn(proj, lower_bound, gains, layer, batch, seq, gh, col0, ts):
    gw = gh * HEAD_DIM
    nt = seq // ts
    blk = lambda k: pl.BlockSpec((ts, gw), lambda b, i, k=k: (b * nt + i, col0 + k))
    return pl.pallas_call(
        functools.partial(_hgrn_kernel, gh=gh),
        grid=(batch, nt),
        in_specs=[
            blk(0), blk(1), blk(2), blk(3),
            pl.BlockSpec((None, 1, gw), lambda b, i: (layer, 0, 0)),
            pl.BlockSpec((None, None, 1, gw), lambda b, i: (layer, 2, 0, 0)),
        ],
        out_specs=pl.BlockSpec((ts, gw), lambda b, i: (b * nt + i, 0)),
        out_shape=jax.ShapeDtypeStruct((batch * seq, gw), BF16),
        scratch_shapes=[pltpu.VMEM((gh, HEAD_DIM, HEAD_DIM), F32), pltpu.VMEM((ts, gw), F32)],
        compiler_params=pltpu.CompilerParams(
            dimension_semantics=("parallel", "arbitrary"), vmem_limit_bytes=VMEM_LIMIT),
        name="hgrn2",
    )(proj, proj, proj, proj, lower_bound, gains)


def _gdn_kernel(xq_ref, xk_ref, xv_ref, z_ref, ab_ref, wq_ref, wk_ref, wv_ref, alog_ref, dt_ref,
                gain_ref, o_ref, pad_ref, act_ref, state_ref, of_ref, *, gh):
    ts = xq_ref.shape[0]
    c = GDN_CHUNK
    first = pl.program_id(1) == 0

    @pl.when(first)
    def _():
        state_ref[...] = jnp.zeros_like(state_ref)
        pad_ref[:, 0:CONV_HALO, :] = jnp.zeros((3, CONV_HALO, pad_ref.shape[2]), F32)

    @pl.when(jnp.logical_not(first))
    def _():
        pad_ref[:, 0:CONV_HALO, :] = pad_ref[:, ts:ts + CONV_HALO, :]

    for part, (x_ref, w_ref) in enumerate(((xq_ref, wq_ref), (xk_ref, wk_ref), (xv_ref, wv_ref))):
        pad_ref[part, CONV_HALO:CONV_HALO + ts, :] = x_ref[...]
        acc = None
        for tap in range(CONV_WIDTH):
            start = CONV_HALO - (CONV_WIDTH - 1) + tap
            term = w_ref[tap:tap + 1, :] * pad_ref[part, start:start + ts, :]
            acc = term if acc is None else acc + term
        act_ref[part] = _silu(acc)

    ab = ab_ref[:, 0:HEAD_DIM]
    log_alpha = -jnp.exp(alog_ref[...]) * _softplus(ab + dt_ref[...])
    beta_all = _sigmoid(ab)

    ii = lax.broadcasted_iota(jnp.int32, (c, c), 0)
    jj = lax.broadcasted_iota(jnp.int32, (c, c), 1)
    incl = jj <= ii
    strict = jj < ii
    prefix = jnp.where(incl, 1.0, 0.0).astype(BF16)
    base_shift = int(np.log2(GDN_INV_BASE))
    in_base = jnp.where((ii >> base_shift) == (jj >> base_shift), 1.0, 0.0)
    merge_masks = []
    for shift in range(base_shift, int(np.log2(c))):
        same_pair = (ii >> (shift + 1)) == (jj >> (shift + 1))
        other_half = (ii >> shift) != (jj >> shift)
        merge_masks.append(jnp.where(same_pair & other_half, 1.0, 0.0))

    n_chunks = ts // c
    bodies = [(ci, h) for ci in range(n_chunks) for h in range(gh)]
    g_all, g_rows = [], []
    for ci in range(n_chunks):
        g = _dot_exact_lhs(prefix, log_alpha[ci * c:(ci + 1) * c])
        g_all.append(g)
        g_rows.append(g.T)

    def load(part, ci, h):
        return act_ref[part, ci * c:(ci + 1) * c, h * HEAD_DIM:(h + 1) * HEAD_DIM]

    def l2norm(t):
        return t * lax.rsqrt(jnp.sum(t * t, axis=-1, keepdims=True) + EPS)

    q = [l2norm(load(0, ci, h)) * (HEAD_DIM ** -0.5) for ci, h in bodies]
    k = [l2norm(load(1, ci, h)) for ci, h in bodies]
    v = [load(2, ci, h) for ci, h in bodies]
    beta = [beta_all[ci * c:(ci + 1) * c, gh + h:gh + h + 1] for ci, h in bodies]
    g_col = [g_all[ci][:, h:h + 1] for ci, h in bodies]
    g_last = [g_all[ci][c - 1:c, h:h + 1] for ci, h in bodies]
    decay = [jnp.where(incl, jnp.exp(jnp.minimum(gc - g_rows[ci][h:h + 1, :], 0.0)), 0.0)
             for gc, (ci, h) in zip(g_col, bodies)]
    k_beta = [ki * bi for ki, bi in zip(k, beta)]
    k16 = [ki.astype(BF16) for ki in k]
    a = [jnp.where(strict, _dot_nt(kb.astype(BF16), kh) * dc, 0.0)
         for kb, kh, dc in zip(k_beta, k16, decay)]
    power = [ai * in_base for ai in a]
    n = [-p for p in power]
    for _ in range(base_shift - 1):
        power = [_dot16(p, p) for p in power]
        n = [ni + p + _dot16(ni, p) for ni, p in zip(n, power)]
    for mask in merge_masks:
        off = [ai * mask for ai in a]
        x = [o + _dot16(ni, o) for ni, o in zip(n, off)]
        n = [ni - (xi + _dot16(xi, ni)) for ni, xi in zip(n, x)]
    rhs = [jnp.concatenate([vi * bi, kb * jnp.exp(gc)], axis=1)
           for vi, bi, kb, gc in zip(v, beta, k_beta, g_col)]
    uw = [r + _dot16(ni, r) for ni, r in zip(n, rhs)]
    qk = [(_dot_nt(qi.astype(BF16), kh) * dc).astype(BF16) for qi, kh, dc in zip(q, k16, decay)]
    q_in = [(qi * jnp.exp(gc)).astype(BF16) for qi, gc in zip(q, g_col)]
    k_out = [(ki * jnp.exp(gl - gc)).astype(BF16) for ki, gl, gc in zip(k, g_last, g_col)]

    for i, (ci, h) in enumerate(bodies):
        state = state_ref[h]
        s16 = state.astype(BF16)
        v_new = uw[i][:, :HEAD_DIM] - _dot(uw[i][:, HEAD_DIM:].astype(BF16), s16)
        vn16 = v_new.astype(BF16)
        o = _dot(q_in[i], s16) + _dot(qk[i], vn16)
        state_ref[h] = jnp.exp(g_last[i]) * state + _dot_tn(k_out[i], vn16)
        of_ref[ci * c:(ci + 1) * c, h * HEAD_DIM:(h + 1) * HEAD_DIM] = o

    for h in range(gh):
        cols = slice(h * HEAD_DIM, (h + 1) * HEAD_DIM)
        o = _rms(of_ref[:, cols], gain_ref[:, cols]) * _silu(z_ref[:, cols])
        o_ref[:, cols] = o.astype(BF16)


def _gdn(proj, conv_w, alog_row, dt_row, gains, layer, batch, seq, gh, col0, ts):
    gw = gh * HEAD_DIM
    nt = seq // ts
    blk = lambda k: pl.BlockSpec((ts, gw), lambda b, i, k=k: (b * nt + i, col0 + k))
    wblk = lambda k: pl.BlockSpec((None, CONV_WIDTH, gw), lambda b, i, k=k: (layer, 0, k))
    row = pl.BlockSpec((None, 1, HEAD_DIM), lambda b, i: (layer, 0, 0))
    return pl.pallas_call(
        functools.partial(_gdn_kernel, gh=gh),
        grid=(batch, nt),
        in_specs=[
            blk(0), blk(1), blk(2), blk(3), blk(4),
            wblk(0), wblk(1), wblk(2), row, row,
            pl.BlockSpec((None, None, 1, gw), lambda b, i: (layer, 3, 0, 0)),
        ],
        out_specs=pl.BlockSpec((ts, gw), lambda b, i: (b * nt + i, 0)),
        out_shape=jax.ShapeDtypeStruct((batch * seq, gw), BF16),
        scratch_shapes=[
            pltpu.VMEM((3, ts + CONV_HALO, gw), F32),
            pltpu.VMEM((3, ts, gw), F32),
            pltpu.VMEM((gh, HEAD_DIM, HEAD_DIM), F32),
            pltpu.VMEM((ts, gw), F32),
        ],
        compiler_params=pltpu.CompilerParams(
            dimension_semantics=("parallel", "arbitrary"), vmem_limit_bytes=VMEM_LIMIT),
        name="gated_deltanet",
    )(proj, proj, proj, proj, proj, conv_w, conv_w, conv_w, alog_row, dt_row, gains)


def _outproj_kernel(x_ref, oa_ref, ob_ref, oc_ref, od_ref, w_ref, gt_ref, gain_ref, o_ref):
    gw = oa_ref.shape[1]
    y = None
    for g, m_ref in enumerate((oa_ref, ob_ref, oc_ref, od_ref)):
        part = _dot(m_ref[...], w_ref[g * gw:(g + 1) * gw, :])
        y = part if y is None else y + part
    o_ref[...] = x_ref[...] + gt_ref[...] * _rms(y, gain_ref[...])


def _outproj(x, groups, w_out, gate, gain, layer, tiles_per_batch, tm):
    n, d = x.shape
    gw = groups[0].shape[1]
    gspec = pl.BlockSpec((tm, gw), lambda i: (i, 0))
    return pl.pallas_call(
        _outproj_kernel,
        grid=(n // tm,),
        in_specs=[
            pl.BlockSpec((tm, d), lambda i: (i, 0)),
            gspec, gspec, gspec, gspec,
            pl.BlockSpec((None, N_GROUPS * gw, d), lambda i: (layer, 0, 0)),
            pl.BlockSpec((None, 1, d), lambda i: (i // tiles_per_batch, 0, 0)),
            pl.BlockSpec((1, d), lambda i: (0, 0)),
        ],
        out_specs=pl.BlockSpec((tm, d), lambda i: (i, 0)),
        out_shape=jax.ShapeDtypeStruct((n, d), F32),
        compiler_params=pltpu.CompilerParams(
            dimension_semantics=("parallel",), vmem_limit_bytes=VMEM_LIMIT),
        name="outproj",
    )(x, *groups, w_out, gate, gain)


def _largest_tile(n, cap):
    t = min(n, cap)
    while n % t:
        t //= 2
    return t


def kernel(x, c, w_mod, b_mod, norm_gain, w_in, w_out, mix_out_gain, hgrn_lb_logits, dn_conv_w,
           dn_a_log, dn_dt_bias, ffn1_w13, ffn1_w2, ffn2_w13, ffn2_w2):
    batch, seq, d = x.shape
    depth = w_mod.shape[0]
    gw = d // N_GROUPS
    gh = gw // HEAD_DIM
    dff = ffn1_w2.shape[1]
    n = batch * seq

    tm_ffn = _largest_tile(seq, 1024)
    tf = dff // 11 if dff % 11 == 0 and (dff // 11) % 128 == 0 else _largest_tile(dff, 512)
    tm_proj = _largest_tile(seq, 1024)
    tm_out = _largest_tile(seq, 512)
    tq = _largest_tile(seq, 256)
    ts_c = _largest_tile(seq, 1024)
    ts_d = _largest_tile(seq, 512)

    in_cols = w_in.shape[2]
    pad_cols = (-in_cols) % gw
    w_in16 = jnp.pad(w_in, ((0, 0), (0, 0), (0, pad_cols))).astype(BF16)
    w_out16 = w_out.astype(BF16)
    ffn_w = [(ffn1_w13.astype(BF16), ffn1_w2.astype(BF16)),
             (ffn2_w13.astype(BF16), ffn2_w2.astype(BF16))]
    lb_p = jax.nn.softmax(hgrn_lb_logits.astype(F32), axis=0)
    lower_bounds = (jnp.cumsum(lb_p, axis=0) - lb_p[0]).reshape(depth, 1, gw)
    gains = mix_out_gain.reshape(depth, N_GROUPS, 1, gw)
    head_gains = mix_out_gain.reshape(depth, N_GROUPS * gh, 1, HEAD_DIM)
    lane_pad = ((0, 0), (0, HEAD_DIM - gh))
    alog_row = jnp.pad(dn_a_log.astype(F32), lane_pad).reshape(depth, 1, HEAD_DIM)
    dt_row = jnp.pad(dn_dt_bias.astype(F32), lane_pad).reshape(depth, 1, HEAD_DIM)
    conv_w = dn_conv_w.astype(F32)

    c_rows = -(-batch // 8) * 8
    c_pad = jnp.pad(c, ((0, c_rows - batch), (0, 0)))
    mod = _modulation(c_pad, w_mod, b_mod, _largest_tile(N_MOD * d, 2048))[:, :batch]
    mod = mod.reshape(depth, batch, N_MOD, 1, d)

    xf = x.reshape(n, d)
    for l in range(depth):
        sh1, sc1, g1, sh2, sc2, g2, sh3, sc3, g3 = (mod[l, :, i] for i in range(N_MOD))
        ng = norm_gain[l].reshape(6, 1, d)
        xf = _ffn(xf, sh1, sc1, g1, ng[0], ng[1], *ffn_w[0], l, seq // tm_ffn, tm_ffn, tf)
        proj = _inproj(xf, sh2, sc2, ng[2], w_in16, l, seq // tm_proj, tm_proj, gw)
        hb = gw // HEAD_DIM
        o_a = _attn_a(proj, head_gains, l, batch, seq, gh, 0 * hb, tq, tq)
        o_b = _attn_b(proj, head_gains, l, batch, seq, gh, 3 * hb, tq)
        o_c = _hgrn(proj, lower_bounds, gains, l, batch, seq, gh, 6, ts_c)
        o_d = _gdn(proj, conv_w, alog_row, dt_row, gains, l, batch, seq, gh, 10, ts_d)
        xf = _outproj(xf, (o_a, o_b, o_c, o_d), w_out16, g2, ng[3], l, seq // tm_out, tm_out)
        xf = _ffn(xf, sh3, sc3, g3, ng[4], ng[5], *ffn_w[1], l, seq // tm_ffn, tm_ffn, tf)
    return xf.reshape(batch, seq, d)
```
